```python
import math
import jax, jax.numpy as jnp
from jax import lax
import numpy as np

D_MODEL = 1024
BATCH = 16
SEQ = 2048
DEPTH = 1

HEAD_DIM = 64
N_HEADS_MOBA = 8
N_HEADS_SB = 8
WIDTH_MOBA = N_HEADS_MOBA * HEAD_DIM
WIDTH_SB = N_HEADS_SB * HEAD_DIM
MOBA_BLOCK = 256
MOBA_TOPK = 3
MOBA_Q_CHUNK = 16
SB_Q_BLOCK = 128
REL_BUCKETS = 32
REL_MAX_DIST = 128
NORM_EPS = 1e-6
NEG_INF = -1e30
PROJ_SPLITS = [WIDTH_MOBA] * 4 + [WIDTH_SB] * 4 + [D_MODEL] * 2
IN_WIDTH = sum(PROJ_SPLITS)
SPLIT_POINTS = [int(c) for c in np.cumsum(PROJ_SPLITS)[:-1]]

kernel_name = "hybrid_moba_stickbreaking_block"


def rms_norm(x, w):
    xf = x.astype(jnp.float32)
    y = xf * lax.rsqrt(jnp.mean(xf * xf, axis=-1, keepdims=True) + NORM_EPS)
    return (y * w.astype(jnp.float32)).astype(x.dtype)


def split_heads(t, n_heads):
    b, s, _ = t.shape
    return t.reshape(b, s, n_heads, HEAD_DIM).transpose(0, 2, 1, 3)


def merge_heads(t):
    b, h, s, dh = t.shape
    return t.transpose(0, 2, 1, 3).reshape(b, s, h * dh)


def t5_bucket(dist):
    n = jnp.maximum(dist, 0)
    max_exact = REL_BUCKETS // 2
    nf = jnp.maximum(n, 1).astype(jnp.float32)
    large = max_exact + (jnp.log(nf / max_exact) / math.log(REL_MAX_DIST / max_exact)
                         * (REL_BUCKETS - max_exact)).astype(jnp.int32)
    large = jnp.minimum(large, REL_BUCKETS - 1)
    return jnp.where(n < max_exact, n, large)


def moba_attention(q, k, v, rel_bias):
    b, h, s, dh = q.shape
    nb = -(-s // MOBA_BLOCK)
    s_pad = nb * MOBA_BLOCK
    top = min(MOBA_TOPK, nb)
    scale = dh ** -0.5
    pad = ((0, 0), (0, 0), (0, s_pad - s), (0, 0))
    q, k, v = jnp.pad(q, pad), jnp.pad(k, pad), jnp.pad(v, pad)
    k_blk = k.reshape(b, h, nb, MOBA_BLOCK, dh)
    v_blk = v.reshape(b, h, nb, MOBA_BLOCK, dh)
    k_mean = jnp.mean(k_blk.astype(jnp.float32), axis=3)
    n_chunks = s_pad // MOBA_Q_CHUNK
    q_chunks = q.reshape(b, h, n_chunks, MOBA_Q_CHUNK, dh).transpose(2, 0, 1, 3, 4)
    bi = jnp.arange(b)[:, None, None, None]
    hi = jnp.arange(h)[None, :, None, None]
    offs = jnp.arange(MOBA_BLOCK)
    blk_ids = jnp.arange(nb)

    def one_chunk(args):
        qc, ci = args
        t = ci * MOBA_Q_CHUNK + jnp.arange(MOBA_Q_CHUNK)
        cur = (ci * MOBA_Q_CHUNK) // MOBA_BLOCK
        gate = jnp.einsum('bhqd,bhnd->bhqn', qc.astype(jnp.float32), k_mean)
        gate = jnp.where(blk_ids < cur, gate, NEG_INF)
        _, idx = lax.top_k(gate, top)
        valid = idx < cur
        k_sel = k_blk[bi, hi, idx]
        v_sel = v_blk[bi, hi, idx]
        pos_sel = idx[..., None] * MOBA_BLOCK + offs
        bias_sel = rel_bias[hi[..., None], t5_bucket(t[:, None, None] - pos_sel)]
        logit_sel = jnp.einsum('bhqd,bhqjkd->bhqjk', qc, k_sel).astype(jnp.float32) * scale + bias_sel
        logit_sel = jnp.where(valid[..., None], logit_sel, NEG_INF).reshape(b, h, MOBA_Q_CHUNK, top * MOBA_BLOCK)
        k_own = lax.dynamic_index_in_dim(k_blk, cur, axis=2, keepdims=False)
        v_own = lax.dynamic_index_in_dim(v_blk, cur, axis=2, keepdims=False)
        dist_own = t[:, None] - (cur * MOBA_BLOCK + offs)[None, :]
        bias_own = rel_bias[:, t5_bucket(dist_own)]
        logit_own = jnp.einsum('bhqd,bhkd->bhqk', qc, k_own).astype(jnp.float32) * scale + bias_own
        logit_own = jnp.where(dist_own >= 0, logit_own, NEG_INF)
        p = jax.nn.softmax(jnp.concatenate([logit_sel, logit_own], axis=-1), axis=-1).astype(v.dtype)
        p_sel = p[..., :top * MOBA_BLOCK].reshape(b, h, MOBA_Q_CHUNK, top, MOBA_BLOCK)
        p_own = p[..., top * MOBA_BLOCK:]
        return (jnp.einsum('bhqjk,bhqjkd->bhqd', p_sel, v_sel)
                + jnp.einsum('bhqk,bhkd->bhqd', p_own, v_own))

    out = lax.map(one_chunk, (q_chunks, jnp.arange(n_chunks)))
    out = out.transpose(1, 2, 0, 3, 4).reshape(b, h, s_pad, dh)
    return out[:, :, :s]


def stick_breaking_attention(q, k, v):
    s_len = q.shape[2]
    scale = q.shape[-1] ** -0.5
    outs = []
    for i in range(s_len // SB_Q_BLOCK):
        t0, t1 = i * SB_Q_BLOCK, (i + 1) * SB_Q_BLOCK
        z = jnp.einsum('bhqd,bhkd->bhqk', q[:, :, t0:t1], k[:, :, :t1]).astype(jnp.float32) * scale
        past = jnp.arange(t1)[None, :] < (t0 + jnp.arange(SB_Q_BLOCK))[:, None]
        log_1m = jnp.where(past, jax.nn.log_sigmoid(-z), 0.0)
        shifted = jnp.concatenate([log_1m[..., 1:], jnp.zeros_like(log_1m[..., :1])], axis=-1)
        suffix = lax.cumsum(shifted, axis=3, reverse=True)
        w = jnp.where(past, jnp.exp(jax.nn.log_sigmoid(z) + suffix), 0.0).astype(v.dtype)
        outs.append(jnp.einsum('bhqk,bhkd->bhqd', w, v[:, :, :t1]))
    return jnp.concatenate(outs, axis=2)


def setup_inputs(seed: int = 0) -> dict:
    key = jax.random.key(seed)
    ks = jax.random.split(key, 10)
    f32 = jnp.float32
    x = jax.random.normal(ks[0], (BATCH, SEQ, D_MODEL), f32)
    norm_w = 1.0 + 0.02 * jax.random.normal(ks[1], (DEPTH, D_MODEL), f32)
    w_in = jax.random.normal(ks[2], (DEPTH, D_MODEL, IN_WIDTH), f32) * D_MODEL ** -0.5
    merge_gate_b = 0.01 * jax.random.normal(ks[3], (DEPTH, 2 * D_MODEL), f32)
    q_norm_w = 1.0 + 0.02 * jax.random.normal(ks[4], (DEPTH, HEAD_DIM), f32)
    k_norm_w = 1.0 + 0.02 * jax.random.normal(ks[5], (DEPTH, HEAD_DIM), f32)
    rel_bias = 0.2 * jax.random.normal(ks[6], (N_HEADS_MOBA, REL_BUCKETS), f32)
    w_up_moba = jax.random.normal(ks[7], (DEPTH, WIDTH_MOBA, D_MODEL), f32) * WIDTH_MOBA ** -0.5
    w_up_sb = jax.random.normal(ks[8], (DEPTH, WIDTH_SB, D_MODEL), f32) * WIDTH_SB ** -0.5
    w_out = jax.random.normal(ks[9], (DEPTH, D_MODEL, D_MODEL), f32) * D_MODEL ** -0.5
    return {"x": x, "norm_w": norm_w, "w_in": w_in, "merge_gate_b": merge_gate_b,
            "q_norm_w": q_norm_w, "k_norm_w": k_norm_w, "rel_bias": rel_bias,
            "w_up_moba": w_up_moba, "w_up_sb": w_up_sb, "w_out": w_out}


def reference(x, norm_w, w_in, merge_gate_b, q_norm_w, k_norm_w, rel_bias, w_up_moba, w_up_sb, w_out):
    for l in range(DEPTH):
        h = rms_norm(x, norm_w[l])
        proj = h @ w_in[l]
        qa, ka, va, za, qb, kb, vb, zb, gl_a, gl_b = jnp.split(proj, SPLIT_POINTS, axis=-1)
        qa = rms_norm(split_heads(qa, N_HEADS_MOBA), q_norm_w[l])
        ka = rms_norm(split_heads(ka, N_HEADS_MOBA), k_norm_w[l])
        oa = moba_attention(qa, ka, split_heads(va, N_HEADS_MOBA), rel_bias)
        ob = stick_breaking_attention(split_heads(qb, N_HEADS_SB), split_heads(kb, N_HEADS_SB),
                                      split_heads(vb, N_HEADS_SB))
        ya = (merge_heads(oa) * jax.nn.silu(za)) @ w_up_moba[l]
        yb = (merge_heads(ob) * jax.nn.silu(zb)) @ w_up_sb[l]
        gb = merge_gate_b[l]
        y = jax.nn.sigmoid(gl_a + gb[:D_MODEL]) * ya + jax.nn.sigmoid(gl_b + gb[D_MODEL:]) * yb
        x = x + y @ w_out[l]
    return x
```

```python
import math

import numpy as np
import jax
import jax.numpy as jnp
from jax import lax
from jax.experimental import pallas as pl
from jax.experimental.pallas import tpu as pltpu

F32 = jnp.float32
BF16 = jnp.bfloat16

HEAD_DIM = 64
N_HEADS = 8
WIDTH = N_HEADS * HEAD_DIM
MOBA_BLOCK = 256
MOBA_TOPK = 3
REL_BUCKETS = 32
REL_MAX_DIST = 128
NORM_EPS = 1e-6
NEG_INF = -1e30
SCALE = HEAD_DIM ** -0.5

LANES = 128
TILE = MOBA_BLOCK
PAIRS = WIDTH // LANES
ONES_ROWS = 16
ATTN_COLS = 8 * WIDTH
VMEM_LIMIT = 48 * 1024 * 1024
PROJ_ROWS = 512
COL_CHUNK = 512


def _dot(a, b):
    return jnp.dot(a, b, preferred_element_type=F32)


def _split_bf16(x):
    hi = x.astype(BF16)
    lo = (x - hi.astype(F32)).astype(BF16)
    return hi, lo


def _rms_rows(x, w):
    ms = jnp.mean(x * x, axis=-1, keepdims=True)
    return x * lax.rsqrt(ms + NORM_EPS) * w


def _proj_kernel(x_ref, nw_ref, w_ref, o_ref, h_ref):
    h_ref[...] = _rms_rows(x_ref[...], nw_ref[...]).astype(BF16)
    for c in range(0, ATTN_COLS, COL_CHUNK):
        o_ref[:, c:c + COL_CHUNK] = _dot(h_ref[...], w_ref[:, c:c + COL_CHUNK]).astype(BF16)


def _project(x2d, norm_w, w_attn):
    m, d = x2d.shape
    return pl.pallas_call(
        _proj_kernel,
        grid=(m // PROJ_ROWS,),
        in_specs=[
            pl.BlockSpec((PROJ_ROWS, d), lambda i: (i, 0)),
            pl.BlockSpec((1, d), lambda i: (0, 0)),
            pl.BlockSpec((d, ATTN_COLS), lambda i: (0, 0)),
        ],
        out_specs=pl.BlockSpec((PROJ_ROWS, ATTN_COLS), lambda i: (i, 0)),
        out_shape=jax.ShapeDtypeStruct((m, ATTN_COLS), BF16),
        scratch_shapes=[pltpu.VMEM((PROJ_ROWS, d), BF16)],
        compiler_params=pltpu.CompilerParams(
            dimension_semantics=("arbitrary",), vmem_limit_bytes=VMEM_LIMIT),
        name="proj",
    )(x2d, norm_w, w_attn)


def _t5_bucket_np(dist):
    n = np.maximum(dist, 0)
    max_exact = REL_BUCKETS // 2
    nf = np.maximum(n, 1).astype(np.float64)
    large = max_exact + (np.log(nf / max_exact) / math.log(REL_MAX_DIST / max_exact)
                         * (REL_BUCKETS - max_exact)).astype(np.int32)
    large = np.minimum(large, REL_BUCKETS - 1)
    return np.where(n < max_exact, n, large).astype(np.int32)


def _bucket_tiles():
    key = np.arange(TILE)[:, None]
    qry = np.arange(TILE)[None, :]
    own = np.where(qry >= key, _t5_bucket_np(qry - key), REL_BUCKETS)
    prev = _t5_bucket_np(TILE + qry - key)
    far = _t5_bucket_np(2 * TILE + qry - key)
    assert (far == REL_BUCKETS - 1).all() and REL_MAX_DIST <= TILE
    return np.stack([own, prev, far]).astype(np.int32)


def _bias_kernel(rel_ref, bkt_ref, o_ref):
    h = pl.program_id(0)
    for t in range(3):
        bkt = bkt_ref[t]
        acc = jnp.full((TILE, TILE), NEG_INF, F32)
        for b in range(REL_BUCKETS):
            acc = jnp.where(bkt == b, rel_ref[h, b], acc)
        o_ref[0, t] = acc


def _bias_tiles(rel_bias):
    return pl.pallas_call(
        _bias_kernel,
        grid=(N_HEADS,),
        in_specs=[
            pl.BlockSpec(memory_space=pltpu.SMEM),
            pl.BlockSpec((3, TILE, TILE), lambda h: (0, 0, 0)),
        ],
        out_specs=pl.BlockSpec((1, 3, TILE, TILE), lambda h: (h, 0, 0, 0)),
        out_shape=jax.ShapeDtypeStruct((N_HEADS, 3, TILE, TILE), F32),
        compiler_params=pltpu.CompilerParams(dimension_semantics=("arbitrary",)),
        name="bias_tiles",
    )(rel_bias, jnp.asarray(_bucket_tiles()))


def _silu_gate_store(o_ref, z_ref, out_t):
    z = z_ref[0].astype(F32)
    o_ref[0] = (out_t.T * z / (1.0 + jnp.exp(-z))).astype(BF16)


def _moba_kernel(q_ref, k_ref, v_ref, z_ref, qw_ref, kw_ref, bias_ref, o_ref,
                 kaug_ref, vt_ref, kmean_ref):
    i = pl.program_id(2)
    n_blocks = k_ref.shape[1] // TILE

    @pl.when(i == 0)
    def _prepare_keys_values():
        lane = lax.broadcasted_iota(jnp.int32, (TILE, LANES), 1)
        head0 = lane < HEAD_DIM
        ones = jnp.ones((ONES_ROWS, TILE), BF16)
        for blk in range(n_blocks):
            rows = slice(blk * TILE, (blk + 1) * TILE)
            kb = k_ref[0, rows, :].astype(F32)
            sq = kb * kb
            ss_all = jnp.sum(sq, axis=-1, keepdims=True)
            ss0 = jnp.sum(jnp.where(head0, sq, 0.0), axis=-1, keepdims=True)
            r0 = lax.rsqrt(ss0 * (1.0 / HEAD_DIM) + NORM_EPS)
            r1 = lax.rsqrt((ss_all - ss0) * (1.0 / HEAD_DIM) + NORM_EPS)
            khat = kb * jnp.where(head0, r0, r1) * kw_ref[...]
            kmean_ref[blk:blk + 1, :] = jnp.mean(khat, axis=0, keepdims=True)
            kaug_ref[0, blk] = jnp.where(head0, khat, (lane == HEAD_DIM + blk).astype(F32)).astype(BF16)
            kaug_ref[1, blk] = jnp.where(head0, (lane == blk).astype(F32), khat).astype(BF16)
            vt = v_ref[0, rows, :].astype(F32).T
            vt_ref[0, blk, 0:HEAD_DIM, :] = vt[0:HEAD_DIM].astype(BF16)
            vt_ref[0, blk, HEAD_DIM:, :] = ones
            vt_ref[1, blk, 0:HEAD_DIM, :] = vt[HEAD_DIM:].astype(BF16)
            vt_ref[1, blk, HEAD_DIM:, :] = ones

    qt = q_ref[0].astype(F32).T
    row = lax.broadcasted_iota(jnp.int32, (LANES, TILE), 0)
    sq = qt * qt
    r0 = lax.rsqrt(jnp.sum(sq[0:HEAD_DIM], axis=0, keepdims=True) * (1.0 / HEAD_DIM) + NORM_EPS)
    r1 = lax.rsqrt(jnp.sum(sq[HEAD_DIM:], axis=0, keepdims=True) * (1.0 / HEAD_DIM) + NORM_EPS)
    qhat = qt * jnp.where(row < HEAD_DIM, r0, r1) * qw_ref[...]

    km = kmean_ref[...]
    lane8 = lax.broadcasted_iota(jnp.int32, km.shape, 1)
    km2 = jnp.concatenate([jnp.where(lane8 < HEAD_DIM, km, 0.0),
                           jnp.where(lane8 < HEAD_DIM, 0.0, km)], axis=0)
    kh, kl = _split_bf16(km2)
    qh, ql = _split_bf16(qhat)
    gate = _dot(kh, qh) + _dot(kh, ql) + _dot(kl, qh)

    def penalty(g):
        n = lax.broadcasted_iota(jnp.int32, g.shape, 0)
        cnt = jnp.zeros(g.shape, jnp.int32)
        for m in range(n_blocks):
            gm = g[m:m + 1, :]
            beats = (gm > g) | ((gm == g) & (m < n))
            cnt = cnt + jnp.where(beats, (m < i).astype(jnp.int32), 0)
        keep = ((cnt < MOBA_TOPK) & (n < i)) | (n == i)
        return jnp.where(keep, 0.0, NEG_INF)

    qs = qhat * SCALE
    pad = jnp.zeros((HEAD_DIM - n_blocks, TILE), F32)
    qaug = (
        jnp.concatenate([qs[0:HEAD_DIM], penalty(gate[0:n_blocks]), pad], axis=0).astype(BF16),
        jnp.concatenate([penalty(gate[n_blocks:]), pad, qs[HEAD_DIM:]], axis=0).astype(BF16),
    )

    def kv_step(jj, carry):
        j = i - jj
        t = jnp.minimum(jj, 2)
        out = []
        for h in range(2):
            m_old, acc = carry[2 * h], carry[2 * h + 1]
            s = _dot(kaug_ref[h, j], qaug[h]) + bias_ref[h, t]
            m_new = jnp.maximum(m_old, jnp.max(s, axis=0, keepdims=True))
            p = jnp.exp(s - m_new)
            acc = jnp.exp(m_old - m_new) * acc + _dot(vt_ref[h, j], p.astype(BF16))
            out += [m_new, acc]
        return tuple(out)

    m_init = jnp.full((1, TILE), -jnp.inf, F32)
    acc_init = jnp.zeros((HEAD_DIM + ONES_ROWS, TILE), F32)
    _, acc0, _, acc1 = lax.fori_loop(0, i + 1, kv_step, (m_init, acc_init, m_init, acc_init))

    out_t = jnp.concatenate([acc0[0:HEAD_DIM] * (1.0 / acc0[HEAD_DIM:HEAD_DIM + 1]),
                             acc1[0:HEAD_DIM] * (1.0 / acc1[HEAD_DIM:HEAD_DIM + 1])], axis=0)
    _silu_gate_store(o_ref, z_ref, out_t)


def _attn_specs(seq, first_col_block):
    c = first_col_block
    return [
        pl.BlockSpec((1, TILE, LANES), lambda b, p, i: (b, i, c + p)),
        pl.BlockSpec((1, seq, LANES), lambda b, p, i: (b, 0, c + PAIRS + p)),
        pl.BlockSpec((1, seq, LANES), lambda b, p, i: (b, 0, c + 2 * PAIRS + p)),
        pl.BlockSpec((1, TILE, LANES), lambda b, p, i: (b, i, c + 3 * PAIRS + p)),
    ]


def _moba(proj, q_norm_w, k_norm_w, bias_tiles):
    bsz, seq, _ = proj.shape
    n_blocks = seq // TILE
    qw = jnp.tile(q_norm_w.astype(F32), 2).reshape(LANES, 1)
    kw = jnp.tile(k_norm_w.astype(F32), 2).reshape(1, LANES)
    return pl.pallas_call(
        _moba_kernel,
        grid=(bsz, PAIRS, n_blocks),
        in_specs=_attn_specs(seq, 0) + [
            pl.BlockSpec((LANES, 1), lambda b, p, i: (0, 0)),
            pl.BlockSpec((1, LANES), lambda b, p, i: (0, 0)),
            pl.BlockSpec((2, 3, TILE, TILE), lambda b, p, i: (p, 0, 0, 0)),
        ],
        out_specs=pl.BlockSpec((1, TILE, LANES), lambda b, p, i: (b, i, p)),
        out_shape=jax.ShapeDtypeStruct((bsz, seq, WIDTH), BF16),
        scratch_shapes=[
            pltpu.VMEM((2, n_blocks, TILE, LANES), BF16),
            pltpu.VMEM((2, n_blocks, HEAD_DIM + ONES_ROWS, TILE), BF16),
            pltpu.VMEM((n_blocks, LANES), F32),
        ],
        compiler_params=pltpu.CompilerParams(
            dimension_semantics=("arbitrary", "arbitrary", "arbitrary"),
            vmem_limit_bytes=VMEM_LIMIT),
        name="moba",
    )(proj, proj, proj, proj, qw, kw, bias_tiles)


def _suffix_matrix():
    s = np.arange(TILE)[:, None]
    j = np.arange(TILE)[None, :]
    tri = np.where(j > s, -1.0, 0.0)
    return np.concatenate([tri, -np.ones((ONES_ROWS, TILE))]).astype(np.float32)


def _sb_kernel(q_ref, k_ref, v_ref, z_ref, tri_ref, o_ref, vt_ref):
    i = pl.program_id(2)
    n_blocks = k_ref.shape[1] // TILE

    @pl.when(i == 0)
    def _transpose_values():
        for blk in range(n_blocks):
            vt = v_ref[0, blk * TILE:(blk + 1) * TILE, :].astype(F32).T
            vt_ref[0, blk] = vt[0:HEAD_DIM].astype(BF16)
            vt_ref[1, blk] = vt[HEAD_DIM:].astype(BF16)

    qt = q_ref[0].astype(F32).T * SCALE
    row = lax.broadcasted_iota(jnp.int32, (LANES, TILE), 0)
    qh = (jnp.where(row < HEAD_DIM, qt, 0.0).astype(BF16),
          jnp.where(row < HEAD_DIM, 0.0, qt).astype(BF16))
    key = lax.broadcasted_iota(jnp.int32, (TILE, TILE), 0)
    qry = lax.broadcasted_iota(jnp.int32, (TILE, TILE), 1)
    past = key < qry

    def kv_tile(j, carry, diagonal):
        kb = k_ref[0, pl.ds(pl.multiple_of(j * TILE, TILE), TILE), :]
        out = []
        for h in range(2):
            later, acc = carry[2 * h], carry[2 * h + 1]
            z = _dot(kb, qh[h])
            nl = jnp.maximum(z, 0.0) + jnp.log(1.0 + jnp.exp(-jnp.abs(z)))
            if diagonal:
                nl = jnp.where(past, nl, 0.0)
            hi, lo = _split_bf16(nl)
            suf = _dot(tri_ref[...], hi) + _dot(tri_ref[...], lo)
            w = jnp.exp((z - nl) + suf[0:TILE] + later)
            if diagonal:
                w = jnp.where(past, w, 0.0)
            acc = acc + _dot(vt_ref[h, j], w.astype(BF16))
            out += [later + suf[TILE:TILE + 1], acc]
        return tuple(out)

    zero_row = jnp.zeros((1, TILE), F32)
    zero_acc = jnp.zeros((HEAD_DIM, TILE), F32)
    carry = kv_tile(i, (zero_row, zero_acc, zero_row, zero_acc), True)
    carry = lax.fori_loop(0, i, lambda jj, c: kv_tile(i - 1 - jj, c, False), carry)
    _silu_gate_store(o_ref, z_ref, jnp.concatenate([carry[1], carry[3]], axis=0))


def _stick_breaking(proj):
    bsz, seq, _ = proj.shape
    n_blocks = seq // TILE
    tri = jnp.asarray(_suffix_matrix(), BF16)
    return pl.pallas_call(
        _sb_kernel,
        grid=(bsz, PAIRS, n_blocks),
        in_specs=_attn_specs(seq, 4 * PAIRS) + [
            pl.BlockSpec((TILE + ONES_ROWS, TILE), lambda b, p, i: (0, 0)),
        ],
        out_specs=pl.BlockSpec((1, TILE, LANES), lambda b, p, i: (b, i, p)),
        out_shape=jax.ShapeDtypeStruct((bsz, seq, WIDTH), BF16),
        scratch_shapes=[pltpu.VMEM((2, n_blocks, HEAD_DIM, TILE), BF16)],
        compiler_params=pltpu.CompilerParams(
            dimension_semantics=("arbitrary", "arbitrary", "arbitrary"),
            vmem_limit_bytes=VMEM_LIMIT),
        name="stick_breaking",
    )(proj, proj, proj, proj, tri)


def _merge_kernel(x_ref, ua_ref, ub_ref, nw_ref, gb_ref, wg_ref, wua_ref, wub_ref, wo_ref,
                  o_ref, h_ref, y_ref):
    d = x_ref.shape[1]
    h_ref[...] = _rms_rows(x_ref[...], nw_ref[...]).astype(BF16)
    for c in range(0, d, COL_CHUNK):
        cols = slice(c, c + COL_CHUNK)
        cols_b = slice(d + c, d + c + COL_CHUNK)
        ga = _dot(h_ref[...], wg_ref[:, cols]) + gb_ref[:, cols]
        gb = _dot(h_ref[...], wg_ref[:, cols_b]) + gb_ref[:, cols_b]
        ya = _dot(ua_ref[...], wua_ref[:, cols])
        yb = _dot(ub_ref[...], wub_ref[:, cols])
        y = ya / (1.0 + jnp.exp(-ga)) + yb / (1.0 + jnp.exp(-gb))
        y_ref[:, cols] = y.astype(BF16)
    for c in range(0, d, COL_CHUNK):
        cols = slice(c, c + COL_CHUNK)
        o_ref[:, cols] = x_ref[:, cols] + _dot(y_ref[...], wo_ref[:, cols])


def _merge(x2d, ua, ub, norm_w, gate_b, w_gate, w_up_a, w_up_b, w_out):
    m, d = x2d.shape
    row_spec = lambda cols: pl.BlockSpec((PROJ_ROWS, cols), lambda i: (i, 0))
    full = lambda a: pl.BlockSpec(a.shape, lambda i: (0, 0))
    return pl.pallas_call(
        _merge_kernel,
        grid=(m // PROJ_ROWS,),
        in_specs=[row_spec(d), row_spec(WIDTH), row_spec(WIDTH), full(norm_w), full(gate_b),
                  full(w_gate), full(w_up_a), full(w_up_b), full(w_out)],
        out_specs=row_spec(d),
        out_shape=jax.ShapeDtypeStruct((m, d), F32),
        scratch_shapes=[pltpu.VMEM((PROJ_ROWS, d), BF16), pltpu.VMEM((PROJ_ROWS, d), BF16)],
        compiler_params=pltpu.CompilerParams(
            dimension_semantics=("arbitrary",), vmem_limit_bytes=VMEM_LIMIT),
        name="merge",
    )(x2d, ua, ub, norm_w, gate_b, w_gate, w_up_a, w_up_b, w_out)


def kernel(x, norm_w, w_in, merge_gate_b, q_norm_w, k_norm_w, rel_bias, w_up_moba, w_up_sb, w_out):
    bsz, seq, d = x.shape
    assert seq % TILE == 0 and (bsz * seq) % PROJ_ROWS == 0 and d % COL_CHUNK == 0
    assert w_in.shape[2] == ATTN_COLS + 2 * d
    bias_tiles = _bias_tiles(rel_bias.astype(F32))
    x2d = x.reshape(bsz * seq, d)
    for l in range(norm_w.shape[0]):
        nw = norm_w[l].reshape(1, d).astype(F32)
        w_l = w_in[l].astype(BF16)
        proj = _project(x2d, nw, w_l[:, :ATTN_COLS]).reshape(bsz, seq, ATTN_COLS)
        ua = _moba(proj, q_norm_w[l], k_norm_w[l], bias_tiles).reshape(bsz * seq, WIDTH)
        ub = _stick_breaking(proj).reshape(bsz * seq, WIDTH)
        x2d = _merge(x2d, ua, ub, nw, merge_gate_b[l].reshape(1, 2 * d).astype(F32),
                     w_l[:, ATTN_COLS:], w_up_moba[l].astype(BF16), w_up_sb[l].astype(BF16),
                     w_out[l].astype(BF16))
    return x2d.reshape(bsz, seq, d)
```

```python
import functools
import math

import numpy as np
import jax
import jax.numpy as jnp
from jax import lax
from jax.experimental import pallas as pl
from jax.experimental.pallas import tpu as pltpu

F32 = jnp.float32
BF16 = jnp.bfloat16

HEAD_DIM = 64
N_HEADS = 8
WIDTH = N_HEADS * HEAD_DIM
MOBA_BLOCK = 256
MOBA_TOPK = 3
REL_BUCKETS = 32
REL_MAX_DIST = 128
NORM_EPS = 1e-6
NEG_INF = -1e30
SCALE = HEAD_DIM ** -0.5
EXP_UNDERFLOW = -104.0

LANES = 128
SUBLANES = 8
TILE = MOBA_BLOCK
PAIRS = WIDTH // LANES
ONES_ROWS = 16
ATTN_COLS = 8 * WIDTH
VMEM_LIMIT = 48 * 1024 * 1024
PROJ_ROWS = 512
COL_CHUNK = 512


def _dot(a, b):
    return jnp.dot(a, b, preferred_element_type=F32)


def _split_bf16(x):
    hi = x.astype(BF16)
    lo = (x - hi.astype(F32)).astype(BF16)
    return hi, lo


def _rms_rows(x, w):
    ms = jnp.mean(x * x, axis=-1, keepdims=True)
    return x * lax.rsqrt(ms + NORM_EPS) * w


def _proj_kernel(x_ref, nw_ref, w_ref, o_ref, h_ref):
    h_ref[...] = _rms_rows(x_ref[...], nw_ref[...]).astype(BF16)
    for c in range(0, ATTN_COLS, COL_CHUNK):
        o_ref[:, c:c + COL_CHUNK] = _dot(h_ref[...], w_ref[:, c:c + COL_CHUNK]).astype(BF16)


def _project(x2d, norm_w, w_attn):
    m, d = x2d.shape
    return pl.pallas_call(
        _proj_kernel,
        grid=(m // PROJ_ROWS,),
        in_specs=[
            pl.BlockSpec((PROJ_ROWS, d), lambda i: (i, 0)),
            pl.BlockSpec((1, d), lambda i: (0, 0)),
            pl.BlockSpec((d, ATTN_COLS), lambda i: (0, 0)),
        ],
        out_specs=pl.BlockSpec((PROJ_ROWS, ATTN_COLS), lambda i: (i, 0)),
        out_shape=jax.ShapeDtypeStruct((m, ATTN_COLS), BF16),
        scratch_shapes=[pltpu.VMEM((PROJ_ROWS, d), BF16)],
        compiler_params=pltpu.CompilerParams(
            dimension_semantics=("arbitrary",), vmem_limit_bytes=VMEM_LIMIT),
        name="proj",
    )(x2d, norm_w, w_attn)


def _t5_bucket_np(dist):
    n = np.maximum(dist, 0)
    max_exact = REL_BUCKETS // 2
    nf = np.maximum(n, 1).astype(np.float64)
    large = max_exact + (np.log(nf / max_exact) / math.log(REL_MAX_DIST / max_exact)
                         * (REL_BUCKETS - max_exact)).astype(np.int32)
    large = np.minimum(large, REL_BUCKETS - 1)
    return np.where(n < max_exact, n, large).astype(np.int32)


def _bucket_tiles():
    key = np.arange(TILE)[:, None]
    qry = np.arange(TILE)[None, :]
    own = np.where(qry >= key, _t5_bucket_np(qry - key), REL_BUCKETS)
    prev = _t5_bucket_np(TILE + qry - key)
    assert (_t5_bucket_np(2 * TILE + qry - key) == REL_BUCKETS - 1).all()
    return np.stack([own, prev]).astype(np.int32)


def _bias_kernel(rel_ref, bkt_ref, o_ref):
    h = pl.program_id(0)
    for t in range(2):
        bkt = bkt_ref[t]
        acc = jnp.full((TILE, TILE), NEG_INF, F32)
        for b in range(REL_BUCKETS):
            acc = jnp.where(bkt == b, rel_ref[h, b], acc)
        o_ref[0, t] = acc


def _bias_tiles(rel_bias):
    return pl.pallas_call(
        _bias_kernel,
        grid=(N_HEADS,),
        in_specs=[
            pl.BlockSpec(memory_space=pltpu.SMEM),
            pl.BlockSpec((2, TILE, TILE), lambda h: (0, 0, 0)),
        ],
        out_specs=pl.BlockSpec((1, 2, TILE, TILE), lambda h: (h, 0, 0, 0)),
        out_shape=jax.ShapeDtypeStruct((N_HEADS, 2, TILE, TILE), F32),
        compiler_params=pltpu.CompilerParams(dimension_semantics=("arbitrary",)),
        name="bias_tiles",
    )(rel_bias, jnp.asarray(_bucket_tiles()))


def _silu_gate_store(o_ref, z_ref, out_t):
    z = z_ref[0].astype(F32)
    o_ref[0] = (out_t.T * z / (1.0 + jnp.exp(-z))).astype(BF16)


def _moba_kernel(rel_ref, q_ref, k_ref, v_ref, z_ref, qw_ref, kw_ref, bias_ref, o_ref,
                 kaug_ref, vt_ref, kmean_ref, qaug_ref):
    p = pl.program_id(1)
    i = pl.program_id(2)
    n_blocks = k_ref.shape[1] // TILE

    @pl.when(i == 0)
    def _prepare_keys_values():
        lane = lax.broadcasted_iota(jnp.int32, (TILE, LANES), 1)
        head0 = lane < HEAD_DIM
        ones = jnp.ones((ONES_ROWS, TILE), BF16)
        for blk in range(n_blocks):
            rows = slice(blk * TILE, (blk + 1) * TILE)
            kb = k_ref[0, rows, :].astype(F32)
            sq = kb * kb
            ss_all = jnp.sum(sq, axis=-1, keepdims=True)
            ss0 = jnp.sum(jnp.where(head0, sq, 0.0), axis=-1, keepdims=True)
            r0 = lax.rsqrt(ss0 * (1.0 / HEAD_DIM) + NORM_EPS)
            r1 = lax.rsqrt((ss_all - ss0) * (1.0 / HEAD_DIM) + NORM_EPS)
            khat = kb * jnp.where(head0, r0, r1) * kw_ref[...]
            kmean_ref[blk:blk + 1, :] = jnp.mean(khat, axis=0, keepdims=True)
            hot0 = ((lane == HEAD_DIM + blk) | (lane == HEAD_DIM + SUBLANES + blk)).astype(F32)
            hot1 = ((lane == blk) | (lane == SUBLANES + blk)).astype(F32)
            kaug_ref[0, blk] = jnp.where(head0, khat, hot0).astype(BF16)
            kaug_ref[1, blk] = jnp.where(head0, hot1, khat).astype(BF16)
            vt = v_ref[0, rows, :].astype(F32).T
            vt_ref[0, blk, 0:HEAD_DIM, :] = vt[0:HEAD_DIM].astype(BF16)
            vt_ref[0, blk, HEAD_DIM:, :] = ones
            vt_ref[1, blk, 0:HEAD_DIM, :] = vt[HEAD_DIM:].astype(BF16)
            vt_ref[1, blk, HEAD_DIM:, :] = ones

    qt = q_ref[0].astype(F32).T
    row = lax.broadcasted_iota(jnp.int32, (LANES, TILE), 0)
    sq = qt * qt
    r0 = lax.rsqrt(jnp.sum(sq[0:HEAD_DIM], axis=0, keepdims=True) * (1.0 / HEAD_DIM) + NORM_EPS)
    r1 = lax.rsqrt(jnp.sum(sq[HEAD_DIM:], axis=0, keepdims=True) * (1.0 / HEAD_DIM) + NORM_EPS)
    qhat = qt * jnp.where(row < HEAD_DIM, r0, r1) * qw_ref[...]

    km = kmean_ref[...]
    lane8 = lax.broadcasted_iota(jnp.int32, km.shape, 1)
    km2 = jnp.concatenate([jnp.where(lane8 < HEAD_DIM, km, 0.0),
                           jnp.where(lane8 < HEAD_DIM, 0.0, km)], axis=0)
    kh, kl = _split_bf16(km2)
    qh, ql = _split_bf16(qhat)
    gate = _dot(kh, qh) + _dot(kh, ql) + _dot(kl, qh)

    def block_terms(g, far_bias):
        n = lax.broadcasted_iota(jnp.int32, g.shape, 0)
        cnt = jnp.zeros(g.shape, jnp.int32)
        for m in range(n_blocks):
            gm = g[m:m + 1, :]
            beats = (gm > g) | ((gm == g) & (m < n))
            cnt = cnt + jnp.where(beats, (m < i).astype(jnp.int32), 0)
        keep = ((cnt < MOBA_TOPK) & (n < i)) | (n == i)
        hi, lo = _split_bf16(jnp.where(keep, jnp.where(n < i - 1, far_bias, 0.0), NEG_INF))
        return [hi.astype(F32), lo.astype(F32)]

    qs = qhat * SCALE
    pad = jnp.zeros((HEAD_DIM - 2 * n_blocks, TILE), F32)
    far0 = rel_ref[2 * p, REL_BUCKETS - 1]
    far1 = rel_ref[2 * p + 1, REL_BUCKETS - 1]
    qaug_ref[0] = jnp.concatenate(
        [qs[0:HEAD_DIM]] + block_terms(gate[0:n_blocks], far0) + [pad], axis=0).astype(BF16)
    qaug_ref[1] = jnp.concatenate(
        block_terms(gate[n_blocks:], far1) + [pad, qs[HEAD_DIM:]], axis=0).astype(BF16)

    def attend(n_past):
        out = []
        for h in range(2):
            qa = qaug_ref[h]
            s = []
            for j in range(n_past + 1):
                sj = _dot(kaug_ref[h, j], qa)
                if n_past - j < 2:
                    sj = sj + bias_ref[h, n_past - j]
                s.append(sj)
            m = functools.reduce(jnp.maximum, [jnp.max(sj, axis=0, keepdims=True) for sj in s])
            acc = None
            for j in range(n_past + 1):
                pv = _dot(vt_ref[h, j], jnp.exp(s[j] - m).astype(BF16))
                acc = pv if acc is None else acc + pv
            out.append(acc[0:HEAD_DIM] * (1.0 / acc[HEAD_DIM:HEAD_DIM + 1]))
        _silu_gate_store(o_ref, z_ref, jnp.concatenate(out, axis=0))

    for c in range(n_blocks):
        pl.when(i == c)(functools.partial(attend, c))


def _attn_specs(seq, first_col_block):
    c = first_col_block
    return [
        pl.BlockSpec((1, TILE, LANES), lambda b, p, i: (b, i, c + p)),
        pl.BlockSpec((1, seq, LANES), lambda b, p, i: (b, 0, c + PAIRS + p)),
        pl.BlockSpec((1, seq, LANES), lambda b, p, i: (b, 0, c + 2 * PAIRS + p)),
        pl.BlockSpec((1, TILE, LANES), lambda b, p, i: (b, i, c + 3 * PAIRS + p)),
    ]


def _moba(proj, q_norm_w, k_norm_w, rel_bias, bias_tiles):
    bsz, seq, _ = proj.shape
    n_blocks = seq // TILE
    assert 2 * n_blocks <= HEAD_DIM
    qw = jnp.tile(q_norm_w.astype(F32), 2).reshape(LANES, 1)
    kw = jnp.tile(k_norm_w.astype(F32), 2).reshape(1, LANES)
    return pl.pallas_call(
        _moba_kernel,
        grid=(bsz, PAIRS, n_blocks),
        in_specs=[pl.BlockSpec(memory_space=pltpu.SMEM)] + _attn_specs(seq, 0) + [
            pl.BlockSpec((LANES, 1), lambda b, p, i: (0, 0)),
            pl.BlockSpec((1, LANES), lambda b, p, i: (0, 0)),
            pl.BlockSpec((2, 2, TILE, TILE), lambda b, p, i: (p, 0, 0, 0)),
        ],
        out_specs=pl.BlockSpec((1, TILE, LANES), lambda b, p, i: (b, i, p)),
        out_shape=jax.ShapeDtypeStruct((bsz, seq, WIDTH), BF16),
        scratch_shapes=[
            pltpu.VMEM((2, n_blocks, TILE, LANES), BF16),
            pltpu.VMEM((2, n_blocks, HEAD_DIM + ONES_ROWS, TILE), BF16),
            pltpu.VMEM((n_blocks, LANES), F32),
            pltpu.VMEM((2, LANES, TILE), BF16),
        ],
        compiler_params=pltpu.CompilerParams(
            dimension_semantics=("arbitrary", "arbitrary", "arbitrary"),
            vmem_limit_bytes=VMEM_LIMIT),
        name="moba",
    )(rel_bias, proj, proj, proj, proj, qw, kw, bias_tiles)


def _suffix_matrix():
    s = np.arange(TILE)[:, None]
    j = np.arange(TILE)[None, :]
    tri = np.where(j > s, -1.0, 0.0)
    return np.concatenate([tri, -np.ones((ONES_ROWS, TILE))]).astype(np.float32)


def _sb_kernel(q_ref, k_ref, v_ref, z_ref, tri_ref, o_ref, vt_ref):
    i = pl.program_id(2)
    n_blocks = k_ref.shape[1] // TILE

    @pl.when(i == 0)
    def _transpose_values():
        for blk in range(n_blocks):
            vt = v_ref[0, blk * TILE:(blk + 1) * TILE, :].astype(F32).T
            vt_ref[0, blk] = vt[0:HEAD_DIM].astype(BF16)
            vt_ref[1, blk] = vt[HEAD_DIM:].astype(BF16)

    qt = q_ref[0].astype(F32).T * SCALE
    row = lax.broadcasted_iota(jnp.int32, (LANES, TILE), 0)
    qh = (jnp.where(row < HEAD_DIM, qt, 0.0).astype(BF16),
          jnp.where(row < HEAD_DIM, 0.0, qt).astype(BF16))
    key = lax.broadcasted_iota(jnp.int32, (TILE, TILE), 0)
    qry = lax.broadcasted_iota(jnp.int32, (TILE, TILE), 1)
    past = key < qry

    def kv_tile(j, h, later, diagonal):
        kb = k_ref[0, pl.ds(pl.multiple_of(j * TILE, TILE), TILE), :]
        z = _dot(kb, qh[h])
        nl = jnp.maximum(z, 0.0) + jnp.log(1.0 + jnp.exp(-jnp.abs(z)))
        if diagonal:
            nl = jnp.where(past, nl, 0.0)
        hi, lo = _split_bf16(nl)
        suf = _dot(tri_ref[...], hi) + _dot(tri_ref[...], lo)
        w = jnp.exp((z - nl) + suf[0:TILE] + later)
        if diagonal:
            w = jnp.where(past, w, 0.0)
        return _dot(vt_ref[h, j], w.astype(BF16)), suf[TILE:TILE + 1]

    prev = jnp.maximum(i - 1, 0)
    prev_off = jnp.where(i > 0, 0.0, -jnp.inf)
    state = []
    for h in range(2):
        acc, later = kv_tile(i, h, 0.0, True)
        pv, total = kv_tile(prev, h, later + prev_off, False)
        state += [later + total, acc + pv]

    def alive(later0, later1):
        return (jnp.maximum(jnp.max(later0), jnp.max(later1)) > EXP_UNDERFLOW).astype(jnp.int32)

    def far_cond(carry):
        return (carry[0] >= 0) & (carry[1] > 0)

    def far_tile(carry):
        j, _, later0, acc0, later1, acc1 = carry
        pv0, total0 = kv_tile(j, 0, later0, False)
        pv1, total1 = kv_tile(j, 1, later1, False)
        later0, later1 = later0 + total0, later1 + total1
        return j - 1, alive(later0, later1), later0, acc0 + pv0, later1, acc1 + pv1

    carry = lax.while_loop(far_cond, far_tile, (i - 2, alive(state[0], state[2]), *state))
    _silu_gate_store(o_ref, z_ref, jnp.concatenate([carry[3], carry[5]], axis=0))


def _stick_breaking(proj):
    bsz, seq, _ = proj.shape
    n_blocks = seq // TILE
    tri = jnp.asarray(_suffix_matrix(), BF16)
    return pl.pallas_call(
        _sb_kernel,
        grid=(bsz, PAIRS, n_blocks),
        in_specs=_attn_specs(seq, 4 * PAIRS) + [
            pl.BlockSpec((TILE + ONES_ROWS, TILE), lambda b, p, i: (0, 0)),
        ],
        out_specs=pl.BlockSpec((1, TILE, LANES), lambda b, p, i: (b, i, p)),
        out_shape=jax.ShapeDtypeStruct((bsz, seq, WIDTH), BF16),
        scratch_shapes=[pltpu.VMEM((2, n_blocks, HEAD_DIM, TILE), BF16)],
        compiler_params=pltpu.CompilerParams(
            dimension_semantics=("arbitrary", "arbitrary", "arbitrary"),
            vmem_limit_bytes=VMEM_LIMIT),
        name="stick_breaking",
    )(proj, proj, proj, proj, tri)


def _merge_kernel(x_ref, ua_ref, ub_ref, nw_ref, gb_ref, wg_ref, wua_ref, wub_ref, wo_ref,
                  o_ref, h_ref, y_ref):
    d = x_ref.shape[1]
    h_ref[...] = _rms_rows(x_ref[...], nw_ref[...]).astype(BF16)
    for c in range(0, d, COL_CHUNK):
        cols = slice(c, c + COL_CHUNK)
        cols_b = slice(d + c, d + c + COL_CHUNK)
        ga = _dot(h_ref[...], wg_ref[:, cols]) + gb_ref[:, cols]
        gb = _dot(h_ref[...], wg_ref[:, cols_b]) + gb_ref[:, cols_b]
        ya = _dot(ua_ref[...], wua_ref[:, cols])
        yb = _dot(ub_ref[...], wub_ref[:, cols])
        y = ya / (1.0 + jnp.exp(-ga)) + yb / (1.0 + jnp.exp(-gb))
        y_ref[:, cols] = y.astype(BF16)
    for c in range(0, d, COL_CHUNK):
        cols = slice(c, c + COL_CHUNK)
        o_ref[:, cols] = x_ref[:, cols] + _dot(y_ref[...], wo_ref[:, cols])


def _merge(x2d, ua, ub, norm_w, gate_b, w_gate, w_up_a, w_up_b, w_out):
    m, d = x2d.shape
    row_spec = lambda cols: pl.BlockSpec((PROJ_ROWS, cols), lambda i: (i, 0))
    full = lambda a: pl.BlockSpec(a.shape, lambda i: (0, 0))
    return pl.pallas_call(
        _merge_kernel,
        grid=(m // PROJ_ROWS,),
        in_specs=[row_spec(d), row_spec(WIDTH), row_spec(WIDTH), full(norm_w), full(gate_b),
                  full(w_gate), full(w_up_a), full(w_up_b), full(w_out)],
        out_specs=row_spec(d),
        out_shape=jax.ShapeDtypeStruct((m, d), F32),
        scratch_shapes=[pltpu.VMEM((PROJ_ROWS, d), BF16), pltpu.VMEM((PROJ_ROWS, d), BF16)],
        compiler_params=pltpu.CompilerParams(
            dimension_semantics=("arbitrary",), vmem_limit_bytes=VMEM_LIMIT),
        name="merge",
    )(x2d, ua, ub, norm_w, gate_b, w_gate, w_up_a, w_up_b, w_out)


def kernel(x, norm_w, w_in, merge_gate_b, q_norm_w, k_norm_w, rel_bias, w_up_moba, w_up_sb, w_out):
    bsz, seq, d = x.shape
    assert seq % TILE == 0 and (bsz * seq) % PROJ_ROWS == 0 and d % COL_CHUNK == 0
    assert w_in.shape[2] == ATTN_COLS + 2 * d
    rel_bias = rel_bias.astype(F32)
    bias_tiles = _bias_tiles(rel_bias)
    x2d = x.reshape(bsz * seq, d)
    for l in range(norm_w.shape[0]):
        nw = norm_w[l].reshape(1, d).astype(F32)
        w_l = w_in[l].astype(BF16)
        proj = _project(x2d, nw, w_l[:, :ATTN_COLS]).reshape(bsz, seq, ATTN_COLS)
        ua = _moba(proj, q_norm_w[l], k_norm_w[l], rel_bias, bias_tiles).reshape(bsz * seq, WIDTH)
        ub = _stick_breaking(proj).reshape(bsz * seq, WIDTH)
        x2d = _merge(x2d, ua, ub, nw, merge_gate_b[l].reshape(1, 2 * d).astype(F32),
                     w_l[:, ATTN_COLS:], w_up_moba[l].astype(BF16), w_up_sb[l].astype(BF16),
                     w_out[l].astype(BF16))
    return x2d.reshape(bsz, seq, d)
```

```python
import functools
import math

import numpy as np
import jax
import jax.numpy as jnp
from jax import lax
from jax.experimental import pallas as pl
from jax.experimental.pallas import tpu as pltpu

F32 = jnp.float32
BF16 = jnp.bfloat16

HEAD_DIM = 64
N_HEADS = 8
WIDTH = N_HEADS * HEAD_DIM
MOBA_BLOCK = 256
MOBA_TOPK = 3
REL_BUCKETS = 32
REL_MAX_DIST = 128
NORM_EPS = 1e-6
NEG_INF = -1e30
SCALE = HEAD_DIM ** -0.5
LOG2_E = math.log2(math.e)
EXP2_UNDERFLOW = -150.0

LANES = 128
SUBLANES = 8
TILE = MOBA_BLOCK
PAIRS = WIDTH // LANES
ONES_ROWS = 16
ATTN_COLS = 8 * WIDTH
VMEM_LIMIT = 48 * 1024 * 1024
PROJ_ROWS = 512
COL_CHUNK = 512


def _dot(a, b):
    return jnp.dot(a, b, preferred_element_type=F32)


def _split_bf16(x):
    hi = x.astype(BF16)
    lo = (x - hi.astype(F32)).astype(BF16)
    return hi, lo


def _rms_rows(x, w):
    ms = jnp.mean(x * x, axis=-1, keepdims=True)
    return x * lax.rsqrt(ms + NORM_EPS) * w


def _proj_kernel(x_ref, nw_ref, w_ref, o_ref, h_ref):
    h_ref[...] = _rms_rows(x_ref[...], nw_ref[...]).astype(BF16)
    for c in range(0, ATTN_COLS, COL_CHUNK):
        o_ref[:, c:c + COL_CHUNK] = _dot(h_ref[...], w_ref[:, c:c + COL_CHUNK]).astype(BF16)


def _project(x2d, norm_w, w_attn):
    m, d = x2d.shape
    return pl.pallas_call(
        _proj_kernel,
        grid=(m // PROJ_ROWS,),
        in_specs=[
            pl.BlockSpec((PROJ_ROWS, d), lambda i: (i, 0)),
            pl.BlockSpec((1, d), lambda i: (0, 0)),
            pl.BlockSpec((d, ATTN_COLS), lambda i: (0, 0)),
        ],
        out_specs=pl.BlockSpec((PROJ_ROWS, ATTN_COLS), lambda i: (i, 0)),
        out_shape=jax.ShapeDtypeStruct((m, ATTN_COLS), BF16),
        scratch_shapes=[pltpu.VMEM((PROJ_ROWS, d), BF16)],
        compiler_params=pltpu.CompilerParams(
            dimension_semantics=("arbitrary",), vmem_limit_bytes=VMEM_LIMIT),
        name="proj",
    )(x2d, norm_w, w_attn)


def _t5_bucket_np(dist):
    n = np.maximum(dist, 0)
    max_exact = REL_BUCKETS // 2
    nf = np.maximum(n, 1).astype(np.float64)
    large = max_exact + (np.log(nf / max_exact) / math.log(REL_MAX_DIST / max_exact)
                         * (REL_BUCKETS - max_exact)).astype(np.int32)
    large = np.minimum(large, REL_BUCKETS - 1)
    return np.where(n < max_exact, n, large).astype(np.int32)


def _bucket_tiles():
    key = np.arange(TILE)[:, None]
    qry = np.arange(TILE)[None, :]
    own = np.where(qry >= key, _t5_bucket_np(qry - key), REL_BUCKETS)
    prev = _t5_bucket_np(TILE + qry - key)
    assert (_t5_bucket_np(2 * TILE + qry - key) == REL_BUCKETS - 1).all()
    return np.stack([own, prev]).astype(np.int32)


def _bias_kernel(rel_ref, bkt_ref, o_ref):
    h = pl.program_id(0)
    for t in range(2):
        bkt = bkt_ref[t]
        acc = jnp.full((TILE, TILE), NEG_INF, F32)
        for b in range(REL_BUCKETS):
            acc = jnp.where(bkt == b, rel_ref[h, b] * LOG2_E, acc)
        o_ref[0, t] = acc


def _bias_tiles(rel_bias):
    return pl.pallas_call(
        _bias_kernel,
        grid=(N_HEADS,),
        in_specs=[
            pl.BlockSpec(memory_space=pltpu.SMEM),
            pl.BlockSpec((2, TILE, TILE), lambda h: (0, 0, 0)),
        ],
        out_specs=pl.BlockSpec((1, 2, TILE, TILE), lambda h: (h, 0, 0, 0)),
        out_shape=jax.ShapeDtypeStruct((N_HEADS, 2, TILE, TILE), F32),
        compiler_params=pltpu.CompilerParams(dimension_semantics=("arbitrary",)),
        name="bias_tiles",
    )(rel_bias, jnp.asarray(_bucket_tiles()))


def _gated_store(o_ref, z_ref, out_t):
    z = z_ref[0].astype(F32)
    gate = z / (1.0 + jnp.exp(-z))
    for p in range(PAIRS):
        lanes = slice(p * LANES, (p + 1) * LANES)
        pair_t = jnp.concatenate([out_t[2 * p], out_t[2 * p + 1]], axis=0)
        o_ref[0, :, lanes] = (pair_t.T * gate[:, lanes]).astype(BF16)


def _moba_kernel(rel_ref, q_ref, k_ref, v_ref, z_ref, qw_ref, kw_ref, bias_ref, o_ref,
                 kaug_ref, vt_ref, kmean_ref, qaug_ref):
    i = pl.program_id(1)
    n_blocks = k_ref.shape[1] // TILE

    @pl.when(i == 0)
    def _prepare_keys_values():
        lane = lax.broadcasted_iota(jnp.int32, (TILE, LANES), 1)
        head0 = lane < HEAD_DIM
        ones = jnp.ones((ONES_ROWS, TILE), BF16)
        for blk in range(n_blocks):
            rows = slice(blk * TILE, (blk + 1) * TILE)
            hot0 = ((lane == HEAD_DIM + blk) | (lane == HEAD_DIM + SUBLANES + blk)).astype(F32)
            hot1 = ((lane == blk) | (lane == SUBLANES + blk)).astype(F32)
            for p in range(PAIRS):
                lanes = slice(p * LANES, (p + 1) * LANES)
                kb = k_ref[0, rows, lanes].astype(F32)
                sq = kb * kb
                ss_all = jnp.sum(sq, axis=-1, keepdims=True)
                ss0 = jnp.sum(jnp.where(head0, sq, 0.0), axis=-1, keepdims=True)
                r0 = lax.rsqrt(ss0 * (1.0 / HEAD_DIM) + NORM_EPS)
                r1 = lax.rsqrt((ss_all - ss0) * (1.0 / HEAD_DIM) + NORM_EPS)
                khat = kb * jnp.where(head0, r0, r1) * kw_ref[...]
                kmean_ref[blk:blk + 1, lanes] = jnp.mean(khat, axis=0, keepdims=True)
                kaug_ref[2 * p, blk] = jnp.where(head0, khat, hot0).astype(BF16)
                kaug_ref[2 * p + 1, blk] = jnp.where(head0, hot1, khat).astype(BF16)
                vt = v_ref[0, rows, lanes].astype(F32).T
                vt_ref[2 * p, blk, 0:HEAD_DIM, :] = vt[0:HEAD_DIM].astype(BF16)
                vt_ref[2 * p, blk, HEAD_DIM:, :] = ones
                vt_ref[2 * p + 1, blk, 0:HEAD_DIM, :] = vt[HEAD_DIM:].astype(BF16)
                vt_ref[2 * p + 1, blk, HEAD_DIM:, :] = ones

    row = lax.broadcasted_iota(jnp.int32, (LANES, TILE), 0)
    top = row < HEAD_DIM
    lane8 = lax.broadcasted_iota(jnp.int32, (n_blocks, LANES), 1)
    pairs = range(PAIRS)
    qts = [q_ref[0, :, p * LANES:(p + 1) * LANES].astype(F32).T for p in pairs]
    qhats = []
    for qt in qts:
        sq = qt * qt
        r0 = lax.rsqrt(jnp.sum(sq[0:HEAD_DIM], axis=0, keepdims=True) * (1.0 / HEAD_DIM) + NORM_EPS)
        r1 = lax.rsqrt(jnp.sum(sq[HEAD_DIM:], axis=0, keepdims=True) * (1.0 / HEAD_DIM) + NORM_EPS)
        qhats.append(qt * jnp.where(top, r0, r1) * qw_ref[...])
    gates = []
    for p, qhat in zip(pairs, qhats):
        km = kmean_ref[:, p * LANES:(p + 1) * LANES]
        km2 = jnp.concatenate([jnp.where(lane8 < HEAD_DIM, km, 0.0),
                               jnp.where(lane8 < HEAD_DIM, 0.0, km)], axis=0)
        kh, kl = _split_bf16(km2)
        qh, ql = _split_bf16(qhat)
        gates.append(_dot(kh, qh) + _dot(kh, ql) + _dot(kl, qh))

    def block_terms(g, far_bias):
        n = lax.broadcasted_iota(jnp.int32, g.shape, 0)
        cnt = jnp.zeros(g.shape, jnp.int32)
        for m in range(n_blocks):
            gm = g[m:m + 1, :]
            beats = (gm > g) | ((gm == g) & (m < n))
            cnt = cnt + jnp.where(beats, (m < i).astype(jnp.int32), 0)
        keep = ((cnt < MOBA_TOPK) & (n < i)) | (n == i)
        hi, lo = _split_bf16(jnp.where(keep, jnp.where(n < i - 1, far_bias, 0.0), NEG_INF))
        return [hi.astype(F32), lo.astype(F32)]

    pad = jnp.zeros((HEAD_DIM - 2 * n_blocks, TILE), F32)
    for p, qhat, gate in zip(pairs, qhats, gates):
        qs = qhat * (SCALE * LOG2_E)
        far0 = rel_ref[2 * p, REL_BUCKETS - 1] * LOG2_E
        far1 = rel_ref[2 * p + 1, REL_BUCKETS - 1] * LOG2_E
        qaug_ref[2 * p] = jnp.concatenate(
            [qs[0:HEAD_DIM]] + block_terms(gate[0:n_blocks], far0) + [pad], axis=0).astype(BF16)
        qaug_ref[2 * p + 1] = jnp.concatenate(
            block_terms(gate[n_blocks:], far1) + [pad, qs[HEAD_DIM:]], axis=0).astype(BF16)

    def attend(n_past):
        s, m, out = {}, {}, {}
        for t in range(N_HEADS + 1):
            if t < N_HEADS:
                qa = qaug_ref[t]
                s[t] = []
                for j in range(n_past + 1):
                    sj = _dot(kaug_ref[t, j], qa)
                    if n_past - j < 2:
                        sj = sj + bias_ref[t, n_past - j]
                    s[t].append(sj)
                m[t] = functools.reduce(jnp.maximum, [jnp.max(sj, axis=0, keepdims=True) for sj in s[t]])
            if t >= 1:
                h = t - 1
                acc = None
                for j, sj in enumerate(s.pop(h)):
                    pv = _dot(vt_ref[h, j], jnp.exp2(sj - m[h]).astype(BF16))
                    acc = pv if acc is None else acc + pv
                out[h] = acc[0:HEAD_DIM] * (1.0 / acc[HEAD_DIM:HEAD_DIM + 1])
        _gated_store(o_ref, z_ref, out)

    for c in range(n_blocks):
        pl.when(i == c)(functools.partial(attend, c))


def _branch_specs(seq, first):
    return [
        pl.BlockSpec((1, TILE, WIDTH), lambda b, i: (b, i, first)),
        pl.BlockSpec((1, seq, WIDTH), lambda b, i: (b, 0, first + 1)),
        pl.BlockSpec((1, seq, WIDTH), lambda b, i: (b, 0, first + 2)),
        pl.BlockSpec((1, TILE, WIDTH), lambda b, i: (b, i, first + 3)),
    ]


def _moba(proj, q_norm_w, k_norm_w, rel_bias, bias_tiles):
    bsz, seq, _ = proj.shape
    n_blocks = seq // TILE
    assert 2 * n_blocks <= HEAD_DIM
    qw = jnp.tile(q_norm_w.astype(F32), 2).reshape(LANES, 1)
    kw = jnp.tile(k_norm_w.astype(F32), 2).reshape(1, LANES)
    return pl.pallas_call(
        _moba_kernel,
        grid=(bsz, n_blocks),
        in_specs=[pl.BlockSpec(memory_space=pltpu.SMEM)] + _branch_specs(seq, 0) + [
            pl.BlockSpec((LANES, 1), lambda b, i: (0, 0)),
            pl.BlockSpec((1, LANES), lambda b, i: (0, 0)),
            pl.BlockSpec(bias_tiles.shape, lambda b, i: (0, 0, 0, 0)),
        ],
        out_specs=pl.BlockSpec((1, TILE, WIDTH), lambda b, i: (b, i, 0)),
        out_shape=jax.ShapeDtypeStruct((bsz, seq, WIDTH), BF16),
        scratch_shapes=[
            pltpu.VMEM((N_HEADS, n_blocks, TILE, LANES), BF16),
            pltpu.VMEM((N_HEADS, n_blocks, HEAD_DIM + ONES_ROWS, TILE), BF16),
            pltpu.VMEM((n_blocks, WIDTH), F32),
            pltpu.VMEM((N_HEADS, LANES, TILE), BF16),
        ],
        compiler_params=pltpu.CompilerParams(
            dimension_semantics=("arbitrary", "arbitrary"), vmem_limit_bytes=VMEM_LIMIT),
        name="moba",
    )(rel_bias, proj, proj, proj, proj, qw, kw, bias_tiles)


def _suffix_matrix():
    s = np.arange(TILE)[:, None]
    j = np.arange(TILE)[None, :]
    return np.concatenate([np.where(j > s, -1.0, 0.0), -np.ones((ONES_ROWS, TILE))]).astype(np.float32)


def _sb_kernel(q_ref, k_ref, v_ref, z_ref, tri_ref, o_ref, vt_ref):
    i = pl.program_id(1)
    n_blocks = k_ref.shape[1] // TILE
    heads = range(N_HEADS)

    @pl.when(i == 0)
    def _transpose_values():
        for blk in range(n_blocks):
            for p in range(PAIRS):
                vt = v_ref[0, blk * TILE:(blk + 1) * TILE, p * LANES:(p + 1) * LANES].astype(F32).T
                vt_ref[2 * p, blk] = vt[0:HEAD_DIM].astype(BF16)
                vt_ref[2 * p + 1, blk] = vt[HEAD_DIM:].astype(BF16)

    row = lax.broadcasted_iota(jnp.int32, (LANES, TILE), 0)
    qh = []
    for p in range(PAIRS):
        qt = q_ref[0, :, p * LANES:(p + 1) * LANES].astype(F32).T * (SCALE * LOG2_E)
        qh += [jnp.where(row < HEAD_DIM, qt, 0.0).astype(BF16),
               jnp.where(row < HEAD_DIM, 0.0, qt).astype(BF16)]
    key = lax.broadcasted_iota(jnp.int32, (TILE, TILE), 0)
    qry = lax.broadcasted_iota(jnp.int32, (TILE, TILE), 1)
    past = key < qry

    def scores(j, h):
        lanes = slice((h // 2) * LANES, (h // 2 + 1) * LANES)
        kb = k_ref[0, pl.ds(pl.multiple_of(j * TILE, TILE), TILE), lanes]
        return _dot(kb, qh[h])

    def softplus2(z2):
        neg_abs = lax.bitcast_convert_type(
            lax.bitcast_convert_type(z2, jnp.uint32) | jnp.uint32(0x80000000), F32)
        return jnp.maximum(z2, 0.0) + jnp.log2(1.0 + jnp.exp2(neg_abs))

    def suffix(nl):
        return _dot(tri_ref[...], nl.astype(BF16))

    prev = jnp.maximum(i - 1, 0)
    prev_off = jnp.where(i > 0, 0.0, -jnp.inf)
    chains = [(h, own) for h in heads for own in (True, False)]
    zs, lw, sufs, pvs = {}, {}, {}, {}
    for t in range(len(chains) + 2):
        if t < len(chains):
            h, own = chains[t]
            zs[t] = scores(i if own else prev, h)
        if 0 <= t - 1 < len(chains):
            c = t - 1
            nl = softplus2(zs[c])
            if chains[c][1]:
                nl = jnp.where(past, nl, 0.0)
            lw[c] = zs.pop(c) - nl
            sufs[c] = suffix(nl)
        if 0 <= t - 2 < len(chains):
            c = t - 2
            h, own = chains[c]
            if own:
                w = jnp.where(past, jnp.exp2(lw.pop(c) + sufs[c][0:TILE]), 0.0)
            else:
                w = jnp.exp2(lw.pop(c) + sufs[c][0:TILE] + (sufs[c - 1][TILE:TILE + 1] + prev_off))
            pvs[c] = _dot(vt_ref[h, i if own else prev], w.astype(BF16))
    later = [sufs[2 * h][TILE:TILE + 1] + sufs[2 * h + 1][TILE:TILE + 1] for h in heads]
    acc = [pvs[2 * h] + pvs[2 * h + 1] for h in heads]

    def alive(laters):
        top = functools.reduce(jnp.maximum, laters)
        return (jnp.max(top) > EXP2_UNDERFLOW).astype(jnp.int32)

    def far_cond(carry):
        return (carry[0] >= 0) & (carry[1] > 0)

    def far_tile(carry):
        j, _, later, acc = carry
        new_later, new_acc = [], []
        for h in heads:
            z = scores(j, h)
            nl = softplus2(z)
            suf = suffix(nl)
            w = jnp.exp2((z - nl) + suf[0:TILE] + later[h])
            new_later.append(later[h] + suf[TILE:TILE + 1])
            new_acc.append(acc[h] + _dot(vt_ref[h, j], w.astype(BF16)))
        return j - 1, alive(new_later), new_later, new_acc

    _, _, _, acc = lax.while_loop(far_cond, far_tile, (i - 2, alive(later), later, acc))
    _gated_store(o_ref, z_ref, acc)


def _stick_breaking(proj):
    bsz, seq, _ = proj.shape
    n_blocks = seq // TILE
    tri = jnp.asarray(_suffix_matrix(), BF16)
    return pl.pallas_call(
        _sb_kernel,
        grid=(bsz, n_blocks),
        in_specs=_branch_specs(seq, 4) + [pl.BlockSpec(tri.shape, lambda b, i: (0, 0))],
        out_specs=pl.BlockSpec((1, TILE, WIDTH), lambda b, i: (b, i, 0)),
        out_shape=jax.ShapeDtypeStruct((bsz, seq, WIDTH), BF16),
        scratch_shapes=[pltpu.VMEM((N_HEADS, n_blocks, HEAD_DIM, TILE), BF16)],
        compiler_params=pltpu.CompilerParams(
            dimension_semantics=("arbitrary", "arbitrary"), vmem_limit_bytes=VMEM_LIMIT),
        name="stick_breaking",
    )(proj, proj, proj, proj, tri)


def _merge_kernel(x_ref, ua_ref, ub_ref, nw_ref, gb_ref, wg_ref, wua_ref, wub_ref, wo_ref,
                  o_ref, h_ref, y_ref):
    d = x_ref.shape[1]
    h_ref[...] = _rms_rows(x_ref[...], nw_ref[...]).astype(BF16)
    for c in range(0, d, COL_CHUNK):
        cols = slice(c, c + COL_CHUNK)
        cols_b = slice(d + c, d + c + COL_CHUNK)
        ga = _dot(h_ref[...], wg_ref[:, cols]) + gb_ref[:, cols]
        gb = _dot(h_ref[...], wg_ref[:, cols_b]) + gb_ref[:, cols_b]
        ya = _dot(ua_ref[...], wua_ref[:, cols])
        yb = _dot(ub_ref[...], wub_ref[:, cols])
        y = ya / (1.0 + jnp.exp(-ga)) + yb / (1.0 + jnp.exp(-gb))
        y_ref[:, cols] = y.astype(BF16)
    for c in range(0, d, COL_CHUNK):
        cols = slice(c, c + COL_CHUNK)
        o_ref[:, cols] = x_ref[:, cols] + _dot(y_ref[...], wo_ref[:, cols])


def _merge(x2d, ua, ub, norm_w, gate_b, w_gate, w_up_a, w_up_b, w_out):
    m, d = x2d.shape
    row_spec = lambda cols: pl.BlockSpec((PROJ_ROWS, cols), lambda i: (i, 0))
    full = lambda a: pl.BlockSpec(a.shape, lambda i: (0, 0))
    return pl.pallas_call(
        _merge_kernel,
        grid=(m // PROJ_ROWS,),
        in_specs=[row_spec(d), row_spec(WIDTH), row_spec(WIDTH), full(norm_w), full(gate_b),
                  full(w_gate), full(w_up_a), full(w_up_b), full(w_out)],
        out_specs=row_spec(d),
        out_shape=jax.ShapeDtypeStruct((m, d), F32),
        scratch_shapes=[pltpu.VMEM((PROJ_ROWS, d), BF16), pltpu.VMEM((PROJ_ROWS, d), BF16)],
        compiler_params=pltpu.CompilerParams(
            dimension_semantics=("arbitrary",), vmem_limit_bytes=VMEM_LIMIT),
        name="merge",
    )(x2d, ua, ub, norm_w, gate_b, w_gate, w_up_a, w_up_b, w_out)


def kernel(x, norm_w, w_in, merge_gate_b, q_norm_w, k_norm_w, rel_bias, w_up_moba, w_up_sb, w_out):
    bsz, seq, d = x.shape
    assert seq % TILE == 0 and (bsz * seq) % PROJ_ROWS == 0 and d % COL_CHUNK == 0
    assert w_in.shape[2] == ATTN_COLS + 2 * d
    rel_bias = rel_bias.astype(F32)
    bias_tiles = _bias_tiles(rel_bias)
    x2d = x.reshape(bsz * seq, d)
    for l in range(norm_w.shape[0]):
        nw = norm_w[l].reshape(1, d).astype(F32)
        w_l = w_in[l].astype(BF16)
        proj = _project(x2d, nw, w_l[:, :ATTN_COLS]).reshape(bsz, seq, ATTN_COLS)
        ua = _moba(proj, q_norm_w[l], k_norm_w[l], rel_bias, bias_tiles).reshape(bsz * seq, WIDTH)
        ub = _stick_breaking(proj).reshape(bsz * seq, WIDTH)
        x2d = _merge(x2d, ua, ub, nw, merge_gate_b[l].reshape(1, 2 * d).astype(F32),
                     w_l[:, ATTN_COLS:], w_up_moba[l].astype(BF16), w_up_sb[l].astype(BF16),
                     w_out[l].astype(BF16))
    return x2d.reshape(bsz, seq, d)
```

```python
import functools
import math

import numpy as np
import jax
import jax.numpy as jnp
from jax import lax
from jax.experimental import pallas as pl
from jax.experimental.pallas import tpu as pltpu

F32 = jnp.float32
BF16 = jnp.bfloat16

HEAD_DIM = 64
N_HEADS = 8
WIDTH = N_HEADS * HEAD_DIM
MOBA_BLOCK = 256
MOBA_TOPK = 3
REL_BUCKETS = 32
REL_MAX_DIST = 128
NORM_EPS = 1e-6
NEG_INF = -1e30
SCALE = HEAD_DIM ** -0.5
LOG2_E = math.log2(math.e)
EXP2_UNDERFLOW = -150.0

LANES = 128
SUBLANES = 8
TILE = MOBA_BLOCK
PAIRS = WIDTH // LANES
ONES_ROWS = 16
ATTN_COLS = 8 * WIDTH
VMEM_LIMIT = 48 * 1024 * 1024
PROJ_ROWS = 512
COL_CHUNK = 512


def _dot(a, b):
    return jnp.dot(a, b, preferred_element_type=F32)


def _split_bf16(x):
    hi = x.astype(BF16)
    lo = (x - hi.astype(F32)).astype(BF16)
    return hi, lo


def _rms_rows(x, w):
    ms = jnp.mean(x * x, axis=-1, keepdims=True)
    return x * lax.rsqrt(ms + NORM_EPS) * w


def _proj_kernel(x_ref, nw_ref, w_ref, o_ref, h_ref):
    h_ref[...] = _rms_rows(x_ref[...], nw_ref[...]).astype(BF16)
    for c in range(0, ATTN_COLS, COL_CHUNK):
        o_ref[:, c:c + COL_CHUNK] = _dot(h_ref[...], w_ref[:, c:c + COL_CHUNK]).astype(BF16)


def _project(x2d, norm_w, w_attn):
    m, d = x2d.shape
    return pl.pallas_call(
        _proj_kernel,
        grid=(m // PROJ_ROWS,),
        in_specs=[
            pl.BlockSpec((PROJ_ROWS, d), lambda i: (i, 0)),
            pl.BlockSpec((1, d), lambda i: (0, 0)),
            pl.BlockSpec((d, ATTN_COLS), lambda i: (0, 0)),
        ],
        out_specs=pl.BlockSpec((PROJ_ROWS, ATTN_COLS), lambda i: (i, 0)),
        out_shape=jax.ShapeDtypeStruct((m, ATTN_COLS), BF16),
        scratch_shapes=[pltpu.VMEM((PROJ_ROWS, d), BF16)],
        compiler_params=pltpu.CompilerParams(
            dimension_semantics=("arbitrary",), vmem_limit_bytes=VMEM_LIMIT),
        name="proj",
    )(x2d, norm_w, w_attn)


def _t5_bucket_np(dist):
    n = np.maximum(dist, 0)
    max_exact = REL_BUCKETS // 2
    nf = np.maximum(n, 1).astype(np.float64)
    large = max_exact + (np.log(nf / max_exact) / math.log(REL_MAX_DIST / max_exact)
                         * (REL_BUCKETS - max_exact)).astype(np.int32)
    large = np.minimum(large, REL_BUCKETS - 1)
    return np.where(n < max_exact, n, large).astype(np.int32)


def _bucket_tiles():
    key = np.arange(TILE)[:, None]
    qry = np.arange(TILE)[None, :]
    own = np.where(qry >= key, _t5_bucket_np(qry - key), REL_BUCKETS)
    prev = _t5_bucket_np(TILE + qry - key)
    assert (_t5_bucket_np(2 * TILE + qry - key) == REL_BUCKETS - 1).all()
    return np.stack([own, prev]).astype(np.int32)


def _bias_kernel(rel_ref, bkt_ref, o_ref):
    h = pl.program_id(0)
    for t in range(2):
        bkt = bkt_ref[t]
        acc = jnp.full((TILE, TILE), NEG_INF, F32)
        for b in range(REL_BUCKETS):
            acc = jnp.where(bkt == b, rel_ref[h, b] * LOG2_E, acc)
        o_ref[0, t] = acc


def _bias_tiles(rel_bias):
    return pl.pallas_call(
        _bias_kernel,
        grid=(N_HEADS,),
        in_specs=[
            pl.BlockSpec(memory_space=pltpu.SMEM),
            pl.BlockSpec((2, TILE, TILE), lambda h: (0, 0, 0)),
        ],
        out_specs=pl.BlockSpec((1, 2, TILE, TILE), lambda h: (h, 0, 0, 0)),
        out_shape=jax.ShapeDtypeStruct((N_HEADS, 2, TILE, TILE), F32),
        compiler_params=pltpu.CompilerParams(dimension_semantics=("arbitrary",)),
        name="bias_tiles",
    )(rel_bias, jnp.asarray(_bucket_tiles()))


def _gated_store(o_ref, z_ref, out_t):
    z = z_ref[0].astype(F32)
    gate = z / (1.0 + jnp.exp(-z))
    for p in range(PAIRS):
        lanes = slice(p * LANES, (p + 1) * LANES)
        pair_t = jnp.concatenate([out_t[2 * p], out_t[2 * p + 1]], axis=0)
        o_ref[0, :, lanes] = (pair_t.T * gate[:, lanes]).astype(BF16)


def _moba_kernel(rel_ref, q_ref, k_ref, v_ref, z_ref, qw_ref, kw_ref, bias_ref, o_ref,
                 kaug_ref, vt_ref, kmean_ref, qaug_ref):
    i = pl.program_id(1)
    n_blocks = k_ref.shape[1] // TILE

    @pl.when(i == 0)
    def _prepare_keys_values():
        lane = lax.broadcasted_iota(jnp.int32, (TILE, LANES), 1)
        head0 = lane < HEAD_DIM
        ones = jnp.ones((ONES_ROWS, TILE), BF16)
        for blk in range(n_blocks):
            rows = slice(blk * TILE, (blk + 1) * TILE)
            hot0 = ((lane == HEAD_DIM + blk) | (lane == HEAD_DIM + SUBLANES + blk)).astype(F32)
            hot1 = ((lane == blk) | (lane == SUBLANES + blk)).astype(F32)
            for p in range(PAIRS):
                lanes = slice(p * LANES, (p + 1) * LANES)
                kb = k_ref[0, rows, lanes].astype(F32)
                sq = kb * kb
                ss_all = jnp.sum(sq, axis=-1, keepdims=True)
                ss0 = jnp.sum(jnp.where(head0, sq, 0.0), axis=-1, keepdims=True)
                r0 = lax.rsqrt(ss0 * (1.0 / HEAD_DIM) + NORM_EPS)
                r1 = lax.rsqrt((ss_all - ss0) * (1.0 / HEAD_DIM) + NORM_EPS)
                khat = kb * jnp.where(head0, r0, r1) * kw_ref[...]
                kmean_ref[blk:blk + 1, lanes] = jnp.mean(khat, axis=0, keepdims=True)
                kaug_ref[2 * p, blk] = jnp.where(head0, khat, hot0).astype(BF16)
                kaug_ref[2 * p + 1, blk] = jnp.where(head0, hot1, khat).astype(BF16)
                vt = v_ref[0, rows, lanes].astype(F32).T
                vt_ref[2 * p, blk, 0:HEAD_DIM, :] = vt[0:HEAD_DIM].astype(BF16)
                vt_ref[2 * p, blk, HEAD_DIM:, :] = ones
                vt_ref[2 * p + 1, blk, 0:HEAD_DIM, :] = vt[HEAD_DIM:].astype(BF16)
                vt_ref[2 * p + 1, blk, HEAD_DIM:, :] = ones

    row = lax.broadcasted_iota(jnp.int32, (LANES, TILE), 0)
    top = row < HEAD_DIM
    lane8 = lax.broadcasted_iota(jnp.int32, (n_blocks, LANES), 1)
    pairs = range(PAIRS)
    qts = [q_ref[0, :, p * LANES:(p + 1) * LANES].astype(F32).T for p in pairs]
    qhats = []
    for qt in qts:
        sq = qt * qt
        r0 = lax.rsqrt(jnp.sum(sq[0:HEAD_DIM], axis=0, keepdims=True) * (1.0 / HEAD_DIM) + NORM_EPS)
        r1 = lax.rsqrt(jnp.sum(sq[HEAD_DIM:], axis=0, keepdims=True) * (1.0 / HEAD_DIM) + NORM_EPS)
        qhats.append(qt * jnp.where(top, r0, r1) * qw_ref[...])
    gates = []
    for p, qhat in zip(pairs, qhats):
        km = kmean_ref[:, p * LANES:(p + 1) * LANES]
        km2 = jnp.concatenate([jnp.where(lane8 < HEAD_DIM, km, 0.0),
                               jnp.where(lane8 < HEAD_DIM, 0.0, km)], axis=0)
        kh, kl = _split_bf16(km2)
        qh, ql = _split_bf16(qhat)
        gates.append(_dot(kh, qh) + _dot(kh, ql) + _dot(kl, qh))

    def block_terms(g, far_bias):
        n = lax.broadcasted_iota(jnp.int32, g.shape, 0)
        cnt = jnp.zeros(g.shape, jnp.int32)
        for m in range(n_blocks):
            gm = g[m:m + 1, :]
            beats = (gm > g) | ((gm == g) & (m < n))
            cnt = cnt + jnp.where(beats, (m < i).astype(jnp.int32), 0)
        keep = ((cnt < MOBA_TOPK) & (n < i)) | (n == i)
        hi, lo = _split_bf16(jnp.where(keep, jnp.where(n < i - 1, far_bias, 0.0), NEG_INF))
        return [hi.astype(F32), lo.astype(F32)]

    pad = jnp.zeros((HEAD_DIM - 2 * n_blocks, TILE), F32)
    for p, qhat, gate in zip(pairs, qhats, gates):
        qs = qhat * (SCALE * LOG2_E)
        far0 = rel_ref[2 * p, REL_BUCKETS - 1] * LOG2_E
        far1 = rel_ref[2 * p + 1, REL_BUCKETS - 1] * LOG2_E
        qaug_ref[2 * p] = jnp.concatenate(
            [qs[0:HEAD_DIM]] + block_terms(gate[0:n_blocks], far0) + [pad], axis=0).astype(BF16)
        qaug_ref[2 * p + 1] = jnp.concatenate(
            block_terms(gate[n_blocks:], far1) + [pad, qs[HEAD_DIM:]], axis=0).astype(BF16)

    def attend(n_past):
        s, p, out = {}, {}, {}
        for t in range(N_HEADS + 2):
            if t < N_HEADS:
                qa = qaug_ref[t]
                s[t] = []
                for j in range(n_past + 1):
                    sj = _dot(kaug_ref[t, j], qa)
                    if n_past - j < 2:
                        sj = sj + bias_ref[t, n_past - j]
                    s[t].append(sj)
            if t >= 2:
                h = t - 2
                acc = None
                for j, pj in enumerate(p.pop(h)):
                    pv = _dot(vt_ref[h, j], pj)
                    acc = pv if acc is None else acc + pv
                out[h] = acc[0:HEAD_DIM] * (1.0 / acc[HEAD_DIM:HEAD_DIM + 1])
            if 1 <= t <= N_HEADS:
                h = t - 1
                m = functools.reduce(jnp.maximum, [jnp.max(sj, axis=0, keepdims=True) for sj in s[h]])
                p[h] = [jnp.exp2(sj - m).astype(BF16) for sj in s.pop(h)]
        _gated_store(o_ref, z_ref, out)

    for c in range(n_blocks):
        pl.when(i == c)(functools.partial(attend, c))


def _branch_specs(seq, first):
    return [
        pl.BlockSpec((1, TILE, WIDTH), lambda b, i: (b, i, first)),
        pl.BlockSpec((1, seq, WIDTH), lambda b, i: (b, 0, first + 1)),
        pl.BlockSpec((1, seq, WIDTH), lambda b, i: (b, 0, first + 2)),
        pl.BlockSpec((1, TILE, WIDTH), lambda b, i: (b, i, first + 3)),
    ]


def _moba(proj, q_norm_w, k_norm_w, rel_bias, bias_tiles):
    bsz, seq, _ = proj.shape
    n_blocks = seq // TILE
    assert 2 * n_blocks <= HEAD_DIM
    qw = jnp.tile(q_norm_w.astype(F32), 2).reshape(LANES, 1)
    kw = jnp.tile(k_norm_w.astype(F32), 2).reshape(1, LANES)
    return pl.pallas_call(
        _moba_kernel,
        grid=(bsz, n_blocks),
        in_specs=[pl.BlockSpec(memory_space=pltpu.SMEM)] + _branch_specs(seq, 0) + [
            pl.BlockSpec((LANES, 1), lambda b, i: (0, 0)),
            pl.BlockSpec((1, LANES), lambda b, i: (0, 0)),
            pl.BlockSpec(bias_tiles.shape, lambda b, i: (0, 0, 0, 0)),
        ],
        out_specs=pl.BlockSpec((1, TILE, WIDTH), lambda b, i: (b, i, 0)),
        out_shape=jax.ShapeDtypeStruct((bsz, seq, WIDTH), BF16),
        scratch_shapes=[
            pltpu.VMEM((N_HEADS, n_blocks, TILE, LANES), BF16),
            pltpu.VMEM((N_HEADS, n_blocks, HEAD_DIM + ONES_ROWS, TILE), BF16),
            pltpu.VMEM((n_blocks, WIDTH), F32),
            pltpu.VMEM((N_HEADS, LANES, TILE), BF16),
        ],
        compiler_params=pltpu.CompilerParams(
            dimension_semantics=("arbitrary", "arbitrary"), vmem_limit_bytes=VMEM_LIMIT),
        name="moba",
    )(rel_bias, proj, proj, proj, proj, qw, kw, bias_tiles)


def _suffix_matrix():
    s = np.arange(TILE)[:, None]
    j = np.arange(TILE)[None, :]
    return np.concatenate([np.where(j > s, -1.0, 0.0), -np.ones((ONES_ROWS, TILE))]).astype(np.float32)


def _sb_kernel(q_ref, k_ref, v_ref, z_ref, tri_ref, o_ref, vt_ref):
    i = pl.program_id(1)
    n_blocks = k_ref.shape[1] // TILE
    heads = range(N_HEADS)

    @pl.when(i == 0)
    def _transpose_values():
        for blk in range(n_blocks):
            for p in range(PAIRS):
                vt = v_ref[0, blk * TILE:(blk + 1) * TILE, p * LANES:(p + 1) * LANES].astype(F32).T
                vt_ref[2 * p, blk] = vt[0:HEAD_DIM].astype(BF16)
                vt_ref[2 * p + 1, blk] = vt[HEAD_DIM:].astype(BF16)

    row = lax.broadcasted_iota(jnp.int32, (LANES, TILE), 0)
    qh = []
    for p in range(PAIRS):
        qt = q_ref[0, :, p * LANES:(p + 1) * LANES].astype(F32).T * (SCALE * LOG2_E)
        qh += [jnp.where(row < HEAD_DIM, qt, 0.0).astype(BF16),
               jnp.where(row < HEAD_DIM, 0.0, qt).astype(BF16)]
    key = lax.broadcasted_iota(jnp.int32, (TILE, TILE), 0)
    qry = lax.broadcasted_iota(jnp.int32, (TILE, TILE), 1)
    past = key < qry

    def scores(j, h):
        lanes = slice((h // 2) * LANES, (h // 2 + 1) * LANES)
        kb = k_ref[0, pl.ds(pl.multiple_of(j * TILE, TILE), TILE), lanes]
        return _dot(kb, qh[h])

    def softplus2(z2):
        neg_abs = lax.bitcast_convert_type(
            lax.bitcast_convert_type(z2, jnp.uint32) | jnp.uint32(0x80000000), F32)
        return jnp.maximum(z2, 0.0) + jnp.log2(1.0 + jnp.exp2(neg_abs))

    def suffix(nl):
        return _dot(tri_ref[...], nl.astype(BF16))

    prev = jnp.maximum(i - 1, 0)
    prev_off = jnp.where(i > 0, 0.0, -jnp.inf)
    chains = [(h, own) for h in heads for own in (True, False)]
    n = len(chains)
    zs, nls, lw, sufs, ws, pvs = {}, {}, {}, {}, {}, {}
    for t in range(n + 4):
        if t < n:
            h, own = chains[t]
            zs[t] = scores(i if own else prev, h)
        if 0 <= t - 2 < n:
            sufs[t - 2] = suffix(nls.pop(t - 2))
        if 0 <= t - 4 < n:
            c = t - 4
            h, own = chains[c]
            pvs[c] = _dot(vt_ref[h, i if own else prev], ws.pop(c))
        if 0 <= t - 1 < n:
            c = t - 1
            nl = softplus2(zs[c])
            if chains[c][1]:
                nl = jnp.where(past, nl, 0.0)
            lw[c] = zs.pop(c) - nl
            nls[c] = nl
        if 0 <= t - 3 < n:
            c = t - 3
            if chains[c][1]:
                w = jnp.where(past, jnp.exp2(lw.pop(c) + sufs[c][0:TILE]), 0.0)
            else:
                w = jnp.exp2(lw.pop(c) + sufs[c][0:TILE] + (sufs[c - 1][TILE:TILE + 1] + prev_off))
            ws[c] = w.astype(BF16)
    later = [sufs[2 * h][TILE:TILE + 1] + sufs[2 * h + 1][TILE:TILE + 1] for h in heads]
    acc = [pvs[2 * h] + pvs[2 * h + 1] for h in heads]

    def alive(laters):
        top = functools.reduce(jnp.maximum, laters)
        return (jnp.max(top) > EXP2_UNDERFLOW).astype(jnp.int32)

    def far_cond(carry):
        return (carry[0] >= 0) & (carry[1] > 0)

    def far_tile(carry):
        j, _, later, acc = carry
        new_later, new_acc = [], []
        for h in heads:
            z = scores(j, h)
            nl = softplus2(z)
            suf = suffix(nl)
            w = jnp.exp2((z - nl) + suf[0:TILE] + later[h])
            new_later.append(later[h] + suf[TILE:TILE + 1])
            new_acc.append(acc[h] + _dot(vt_ref[h, j], w.astype(BF16)))
        return j - 1, alive(new_later), new_later, new_acc

    _, _, _, acc = lax.while_loop(far_cond, far_tile, (i - 2, alive(later), later, acc))
    _gated_store(o_ref, z_ref, acc)


def _stick_breaking(proj):
    bsz, seq, _ = proj.shape
    n_blocks = seq // TILE
    tri = jnp.asarray(_suffix_matrix(), BF16)
    return pl.pallas_call(
        _sb_kernel,
        grid=(bsz, n_blocks),
        in_specs=_branch_specs(seq, 4) + [pl.BlockSpec(tri.shape, lambda b, i: (0, 0))],
        out_specs=pl.BlockSpec((1, TILE, WIDTH), lambda b, i: (b, i, 0)),
        out_shape=jax.ShapeDtypeStruct((bsz, seq, WIDTH), BF16),
        scratch_shapes=[pltpu.VMEM((N_HEADS, n_blocks, HEAD_DIM, TILE), BF16)],
        compiler_params=pltpu.CompilerParams(
            dimension_semantics=("arbitrary", "arbitrary"), vmem_limit_bytes=VMEM_LIMIT),
        name="stick_breaking",
    )(proj, proj, proj, proj, tri)


def _merge_kernel(x_ref, ua_ref, ub_ref, nw_ref, gb_ref, wg_ref, wua_ref, wub_ref, wo_ref,
                  o_ref, h_ref, y_ref):
    d = x_ref.shape[1]
    h_ref[...] = _rms_rows(x_ref[...], nw_ref[...]).astype(BF16)
    for c in range(0, d, COL_CHUNK):
        cols = slice(c, c + COL_CHUNK)
        cols_b = slice(d + c, d + c + COL_CHUNK)
        ga = _dot(h_ref[...], wg_ref[:, cols]) + gb_ref[:, cols]
        gb = _dot(h_ref[...], wg_ref[:, cols_b]) + gb_ref[:, cols_b]
        ya = _dot(ua_ref[...], wua_ref[:, cols])
        yb = _dot(ub_ref[...], wub_ref[:, cols])
        y = ya / (1.0 + jnp.exp(-ga)) + yb / (1.0 + jnp.exp(-gb))
        y_ref[:, cols] = y.astype(BF16)
    for c in range(0, d, COL_CHUNK):
        cols = slice(c, c + COL_CHUNK)
        o_ref[:, cols] = x_ref[:, cols] + _dot(y_ref[...], wo_ref[:, cols])


def _merge(x2d, ua, ub, norm_w, gate_b, w_gate, w_up_a, w_up_b, w_out):
    m, d = x2d.shape
    row_spec = lambda cols: pl.BlockSpec((PROJ_ROWS, cols), lambda i: (i, 0))
    full = lambda a: pl.BlockSpec(a.shape, lambda i: (0, 0))
    return pl.pallas_call(
        _merge_kernel,
        grid=(m // PROJ_ROWS,),
        in_specs=[row_spec(d), row_spec(WIDTH), row_spec(WIDTH), full(norm_w), full(gate_b),
                  full(w_gate), full(w_up_a), full(w_up_b), full(w_out)],
        out_specs=row_spec(d),
        out_shape=jax.ShapeDtypeStruct((m, d), F32),
        scratch_shapes=[pltpu.VMEM((PROJ_ROWS, d), BF16), pltpu.VMEM((PROJ_ROWS, d), BF16)],
        compiler_params=pltpu.CompilerParams(
            dimension_semantics=("arbitrary",), vmem_limit_bytes=VMEM_LIMIT),
        name="merge",
    )(x2d, ua, ub, norm_w, gate_b, w_gate, w_up_a, w_up_b, w_out)


def kernel(x, norm_w, w_in, merge_gate_b, q_norm_w, k_norm_w, rel_bias, w_up_moba, w_up_sb, w_out):
    bsz, seq, d = x.shape
    assert seq % TILE == 0 and (bsz * seq) % PROJ_ROWS == 0 and d % COL_CHUNK == 0
    assert w_in.shape[2] == ATTN_COLS + 2 * d
    rel_bias = rel_bias.astype(F32)
    bias_tiles = _bias_tiles(rel_bias)
    x2d = x.reshape(bsz * seq, d)
    for l in range(norm_w.shape[0]):
        nw = norm_w[l].reshape(1, d).astype(F32)
        w_l = w_in[l].astype(BF16)
        proj = _project(x2d, nw, w_l[:, :ATTN_COLS]).reshape(bsz, seq, ATTN_COLS)
        ua = _moba(proj, q_norm_w[l], k_norm_w[l], rel_bias, bias_tiles).reshape(bsz * seq, WIDTH)
        ub = _stick_breaking(proj).reshape(bsz * seq, WIDTH)
        x2d = _merge(x2d, ua, ub, nw, merge_gate_b[l].reshape(1, 2 * d).astype(F32),
                     w_l[:, ATTN_COLS:], w_up_moba[l].astype(BF16), w_up_sb[l].astype(BF16),
                     w_out[l].astype(BF16))
    return x2d.reshape(bsz, seq, d)
```

```python
import functools
import math

import numpy as np
import jax
import jax.numpy as jnp
from jax import lax
from jax.experimental import pallas as pl
from jax.experimental.pallas import tpu as pltpu

F32 = jnp.float32
BF16 = jnp.bfloat16

HEAD_DIM = 64
N_HEADS = 8
WIDTH = N_HEADS * HEAD_DIM
MOBA_BLOCK = 256
MOBA_TOPK = 3
REL_BUCKETS = 32
REL_MAX_DIST = 128
NORM_EPS = 1e-6
NEG_INF = -1e30
SCALE = HEAD_DIM ** -0.5
LOG2_E = math.log2(math.e)
EXP2_UNDERFLOW = -150.0

LANES = 128
SUBLANES = 8
TILE = MOBA_BLOCK
PAIRS = WIDTH // LANES
ONES_ROWS = 16
ATTN_COLS = 8 * WIDTH
VMEM_LIMIT = 48 * 1024 * 1024
PROJ_ROWS = 512
COL_CHUNK = 512


def _dot(a, b):
    return jnp.dot(a, b, preferred_element_type=F32)


def _split_bf16(x):
    hi = x.astype(BF16)
    lo = (x - hi.astype(F32)).astype(BF16)
    return hi, lo


def _rms_rows(x, w):
    ms = jnp.mean(x * x, axis=-1, keepdims=True)
    return x * lax.rsqrt(ms + NORM_EPS) * w


def _proj_kernel(x_ref, nw_ref, w_ref, o_ref, h_ref):
    h_ref[...] = _rms_rows(x_ref[...], nw_ref[...]).astype(BF16)
    for c in range(0, ATTN_COLS, COL_CHUNK):
        o_ref[:, c:c + COL_CHUNK] = _dot(h_ref[...], w_ref[:, c:c + COL_CHUNK]).astype(BF16)


def _project(x2d, norm_w, w_attn):
    m, d = x2d.shape
    return pl.pallas_call(
        _proj_kernel,
        grid=(m // PROJ_ROWS,),
        in_specs=[
            pl.BlockSpec((PROJ_ROWS, d), lambda i: (i, 0)),
            pl.BlockSpec((1, d), lambda i: (0, 0)),
            pl.BlockSpec((d, ATTN_COLS), lambda i: (0, 0)),
        ],
        out_specs=pl.BlockSpec((PROJ_ROWS, ATTN_COLS), lambda i: (i, 0)),
        out_shape=jax.ShapeDtypeStruct((m, ATTN_COLS), BF16),
        scratch_shapes=[pltpu.VMEM((PROJ_ROWS, d), BF16)],
        compiler_params=pltpu.CompilerParams(
            dimension_semantics=("arbitrary",), vmem_limit_bytes=VMEM_LIMIT),
        name="proj",
    )(x2d, norm_w, w_attn)


def _t5_bucket_np(dist):
    n = np.maximum(dist, 0)
    max_exact = REL_BUCKETS // 2
    nf = np.maximum(n, 1).astype(np.float64)
    large = max_exact + (np.log(nf / max_exact) / math.log(REL_MAX_DIST / max_exact)
                         * (REL_BUCKETS - max_exact)).astype(np.int32)
    large = np.minimum(large, REL_BUCKETS - 1)
    return np.where(n < max_exact, n, large).astype(np.int32)


def _bucket_tiles():
    key = np.arange(TILE)[:, None]
    qry = np.arange(TILE)[None, :]
    own = np.where(qry >= key, _t5_bucket_np(qry - key), REL_BUCKETS)
    prev = _t5_bucket_np(TILE + qry - key)
    assert (_t5_bucket_np(2 * TILE + qry - key) == REL_BUCKETS - 1).all()
    return np.stack([own, prev]).astype(np.int32)


def _bias_kernel(rel_ref, bkt_ref, o_ref):
    h = pl.program_id(0)
    for t in range(2):
        bkt = bkt_ref[t]
        acc = jnp.full((TILE, TILE), NEG_INF, F32)
        for b in range(REL_BUCKETS):
            acc = jnp.where(bkt == b, rel_ref[h, b] * LOG2_E, acc)
        o_ref[0, t] = acc


def _bias_tiles(rel_bias):
    return pl.pallas_call(
        _bias_kernel,
        grid=(N_HEADS,),
        in_specs=[
            pl.BlockSpec(memory_space=pltpu.SMEM),
            pl.BlockSpec((2, TILE, TILE), lambda h: (0, 0, 0)),
        ],
        out_specs=pl.BlockSpec((1, 2, TILE, TILE), lambda h: (h, 0, 0, 0)),
        out_shape=jax.ShapeDtypeStruct((N_HEADS, 2, TILE, TILE), F32),
        compiler_params=pltpu.CompilerParams(dimension_semantics=("arbitrary",)),
        name="bias_tiles",
    )(rel_bias, jnp.asarray(_bucket_tiles()))


def _gated_store(o_ref, z_ref, out_t):
    z = z_ref[0].astype(F32)
    gate = z / (1.0 + jnp.exp(-z))
    for p in range(PAIRS):
        lanes = slice(p * LANES, (p + 1) * LANES)
        pair_t = jnp.concatenate([out_t[2 * p], out_t[2 * p + 1]], axis=0)
        o_ref[0, :, lanes] = (pair_t.T * gate[:, lanes]).astype(BF16)


def _suffix_matrix():
    s = np.arange(TILE)[:, None]
    j = np.arange(TILE)[None, :]
    return np.concatenate([np.where(j > s, -1.0, 0.0), -np.ones((ONES_ROWS, TILE))]).astype(np.float32)


def _interleave(*stage_lists):
    todo = [list(s) for s in stage_lists]
    done = [0] * len(todo)
    while any(done[k] < len(todo[k]) for k in range(len(todo))):
        k = min((k for k in range(len(todo)) if done[k] < len(todo[k])),
                key=lambda k: (done[k] + 0.5) / len(todo[k]))
        todo[k][done[k]]()
        done[k] += 1


def _attn_kernel(rel_ref, qa_ref, ka_ref, va_ref, za_ref, qb_ref, kb_ref, vb_ref, zb_ref,
                 qw_ref, kw_ref, bias_ref, tri_ref, oa_ref, ob_ref,
                 kaug_ref, vta_ref, kmean_ref, qaug_ref, vtb_ref, qsb_ref, later_ref, acc_ref):
    i = pl.program_id(1)
    n_blocks = ka_ref.shape[1] // TILE
    heads = range(N_HEADS)
    pairs = range(PAIRS)

    @pl.when(i == 0)
    def _prepare_keys_values():
        lane = lax.broadcasted_iota(jnp.int32, (TILE, LANES), 1)
        head0 = lane < HEAD_DIM
        ones = jnp.ones((ONES_ROWS, TILE), BF16)
        for blk in range(n_blocks):
            rows = slice(blk * TILE, (blk + 1) * TILE)
            hot0 = ((lane == HEAD_DIM + blk) | (lane == HEAD_DIM + SUBLANES + blk)).astype(F32)
            hot1 = ((lane == blk) | (lane == SUBLANES + blk)).astype(F32)
            for p in pairs:
                lanes = slice(p * LANES, (p + 1) * LANES)
                kb = ka_ref[0, rows, lanes].astype(F32)
                sq = kb * kb
                ss_all = jnp.sum(sq, axis=-1, keepdims=True)
                ss0 = jnp.sum(jnp.where(head0, sq, 0.0), axis=-1, keepdims=True)
                r0 = lax.rsqrt(ss0 * (1.0 / HEAD_DIM) + NORM_EPS)
                r1 = lax.rsqrt((ss_all - ss0) * (1.0 / HEAD_DIM) + NORM_EPS)
                khat = kb * jnp.where(head0, r0, r1) * kw_ref[...]
                kmean_ref[blk:blk + 1, lanes] = jnp.mean(khat, axis=0, keepdims=True)
                kaug_ref[2 * p, blk] = jnp.where(head0, khat, hot0).astype(BF16)
                kaug_ref[2 * p + 1, blk] = jnp.where(head0, hot1, khat).astype(BF16)
                vt = va_ref[0, rows, lanes].astype(F32).T
                vta_ref[2 * p, blk, 0:HEAD_DIM, :] = vt[0:HEAD_DIM].astype(BF16)
                vta_ref[2 * p, blk, HEAD_DIM:, :] = ones
                vta_ref[2 * p + 1, blk, 0:HEAD_DIM, :] = vt[HEAD_DIM:].astype(BF16)
                vta_ref[2 * p + 1, blk, HEAD_DIM:, :] = ones
                vt = vb_ref[0, rows, lanes].astype(F32).T
                vtb_ref[2 * p, blk] = vt[0:HEAD_DIM].astype(BF16)
                vtb_ref[2 * p + 1, blk] = vt[HEAD_DIM:].astype(BF16)

    row = lax.broadcasted_iota(jnp.int32, (LANES, TILE), 0)
    top = row < HEAD_DIM
    lane8 = lax.broadcasted_iota(jnp.int32, (n_blocks, LANES), 1)
    qts = [qa_ref[0, :, p * LANES:(p + 1) * LANES].astype(F32).T for p in pairs]
    qhats = []
    for qt in qts:
        sq = qt * qt
        r0 = lax.rsqrt(jnp.sum(sq[0:HEAD_DIM], axis=0, keepdims=True) * (1.0 / HEAD_DIM) + NORM_EPS)
        r1 = lax.rsqrt(jnp.sum(sq[HEAD_DIM:], axis=0, keepdims=True) * (1.0 / HEAD_DIM) + NORM_EPS)
        qhats.append(qt * jnp.where(top, r0, r1) * qw_ref[...])
    gates = []
    for p, qhat in zip(pairs, qhats):
        km = kmean_ref[:, p * LANES:(p + 1) * LANES]
        km2 = jnp.concatenate([jnp.where(lane8 < HEAD_DIM, km, 0.0),
                               jnp.where(lane8 < HEAD_DIM, 0.0, km)], axis=0)
        kh, kl = _split_bf16(km2)
        qh, ql = _split_bf16(qhat)
        gates.append(_dot(kh, qh) + _dot(kh, ql) + _dot(kl, qh))
    for p in pairs:
        qt = qb_ref[0, :, p * LANES:(p + 1) * LANES].astype(F32).T * (SCALE * LOG2_E)
        qsb_ref[2 * p] = jnp.where(top, qt, 0.0).astype(BF16)
        qsb_ref[2 * p + 1] = jnp.where(top, 0.0, qt).astype(BF16)

    def block_terms(g, far_bias):
        n = lax.broadcasted_iota(jnp.int32, g.shape, 0)
        cnt = jnp.zeros(g.shape, jnp.int32)
        for m in range(n_blocks):
            gm = g[m:m + 1, :]
            beats = (gm > g) | ((gm == g) & (m < n))
            cnt = cnt + jnp.where(beats, (m < i).astype(jnp.int32), 0)
        keep = ((cnt < MOBA_TOPK) & (n < i)) | (n == i)
        hi, lo = _split_bf16(jnp.where(keep, jnp.where(n < i - 1, far_bias, 0.0), NEG_INF))
        return [hi.astype(F32), lo.astype(F32)]

    pad = jnp.zeros((HEAD_DIM - 2 * n_blocks, TILE), F32)
    for p, qhat, gate in zip(pairs, qhats, gates):
        qs = qhat * (SCALE * LOG2_E)
        far0 = rel_ref[2 * p, REL_BUCKETS - 1] * LOG2_E
        far1 = rel_ref[2 * p + 1, REL_BUCKETS - 1] * LOG2_E
        qaug_ref[2 * p] = jnp.concatenate(
            [qs[0:HEAD_DIM]] + block_terms(gate[0:n_blocks], far0) + [pad], axis=0).astype(BF16)
        qaug_ref[2 * p + 1] = jnp.concatenate(
            block_terms(gate[n_blocks:], far1) + [pad, qs[HEAD_DIM:]], axis=0).astype(BF16)

    def sb_scores(j, h):
        lanes = slice((h // 2) * LANES, (h // 2 + 1) * LANES)
        if isinstance(j, int):
            kb = kb_ref[0, j * TILE:(j + 1) * TILE, lanes]
        else:
            kb = kb_ref[0, pl.ds(pl.multiple_of(j * TILE, TILE), TILE), lanes]
        return _dot(kb, qsb_ref[h])

    def softplus2(z2):
        neg_abs = lax.bitcast_convert_type(
            lax.bitcast_convert_type(z2, jnp.uint32) | jnp.uint32(0x80000000), F32)
        return jnp.maximum(z2, 0.0) + jnp.log2(1.0 + jnp.exp2(neg_abs))

    def suffix(nl):
        return _dot(tri_ref[...], nl.astype(BF16))

    def moba_stages(n_past):
        s, p, out = {}, {}, {}

        def step(t):
            if t < N_HEADS:
                qa = qaug_ref[t]
                s[t] = []
                for j in range(n_past + 1):
                    sj = _dot(kaug_ref[t, j], qa)
                    if n_past - j < 2:
                        sj = sj + bias_ref[t, n_past - j]
                    s[t].append(sj)
            if t >= 2:
                h = t - 2
                acc = None
                for j, pj in enumerate(p.pop(h)):
                    pv = _dot(vta_ref[h, j], pj)
                    acc = pv if acc is None else acc + pv
                out[h] = acc[0:HEAD_DIM] * (1.0 / acc[HEAD_DIM:HEAD_DIM + 1])
            if 1 <= t <= N_HEADS:
                h = t - 1
                m = functools.reduce(jnp.maximum, [jnp.max(sj, axis=0, keepdims=True) for sj in s[h]])
                p[h] = [jnp.exp2(sj - m).astype(BF16) for sj in s.pop(h)]
            if t == N_HEADS + 1:
                _gated_store(oa_ref, za_ref, out)

        return [functools.partial(step, t) for t in range(N_HEADS + 2)]

    def sb_stages(own_blk):
        key = lax.broadcasted_iota(jnp.int32, (TILE, TILE), 0)
        qry = lax.broadcasted_iota(jnp.int32, (TILE, TILE), 1)
        past = key < qry
        tiles = (own_blk, own_blk - 1) if own_blk > 0 else (own_blk,)
        chains = [(h, j) for h in heads for j in tiles]
        n = len(chains)
        zs, nls, lw, sufs, ws, pvs = {}, {}, {}, {}, {}, {}

        def step(t):
            if t < n:
                h, j = chains[t]
                zs[t] = sb_scores(j, h)
            if 0 <= t - 2 < n:
                sufs[t - 2] = suffix(nls.pop(t - 2))
            if 0 <= t - 4 < n:
                c = t - 4
                h, j = chains[c]
                pvs[c] = _dot(vtb_ref[h, j], ws.pop(c))
            if 0 <= t - 1 < n:
                c = t - 1
                nl = softplus2(zs[c])
                if chains[c][1] == own_blk:
                    nl = jnp.where(past, nl, 0.0)
                lw[c] = zs.pop(c) - nl
                nls[c] = nl
            if 0 <= t - 3 < n:
                c = t - 3
                if chains[c][1] == own_blk:
                    w = jnp.where(past, jnp.exp2(lw.pop(c) + sufs[c][0:TILE]), 0.0)
                else:
                    w = jnp.exp2(lw.pop(c) + sufs[c][0:TILE] + sufs[c - 1][TILE:TILE + 1])
                ws[c] = w.astype(BF16)
            if t == n + 3:
                per = len(tiles)
                for h in heads:
                    later_ref[h] = sum(sufs[per * h + k][TILE:TILE + SUBLANES] for k in range(per))
                    acc_ref[h] = sum(pvs[per * h + k] for k in range(per))

        return [functools.partial(step, t) for t in range(n + 4)]

    for c in range(n_blocks):
        pl.when(i == c)(functools.partial(lambda c: _interleave(moba_stages(c), sb_stages(c)), c))

    def alive(laters):
        top_later = functools.reduce(jnp.maximum, laters)
        return (jnp.max(top_later) > EXP2_UNDERFLOW).astype(jnp.int32)

    def far_cond(carry):
        return (carry[0] >= 0) & (carry[1] > 0)

    def far_tile(carry):
        j, _, later, acc = carry
        new_later, new_acc = [], []
        for h in heads:
            z = sb_scores(j, h)
            nl = softplus2(z)
            suf = suffix(nl)
            w = jnp.exp2((z - nl) + suf[0:TILE] + later[h])
            new_later.append(later[h] + suf[TILE:TILE + 1])
            new_acc.append(acc[h] + _dot(vtb_ref[h, j], w.astype(BF16)))
        return j - 1, alive(new_later), new_later, new_acc

    later = [later_ref[h][0:1] for h in heads]
    acc = [acc_ref[h] for h in heads]
    _, _, _, acc = lax.while_loop(far_cond, far_tile, (i - 2, alive(later), later, acc))
    _gated_store(ob_ref, zb_ref, acc)


def _branch_specs(seq, first):
    return [
        pl.BlockSpec((1, TILE, WIDTH), lambda b, i: (b, i, first)),
        pl.BlockSpec((1, seq, WIDTH), lambda b, i: (b, 0, first + 1)),
        pl.BlockSpec((1, seq, WIDTH), lambda b, i: (b, 0, first + 2)),
        pl.BlockSpec((1, TILE, WIDTH), lambda b, i: (b, i, first + 3)),
    ]


def _attention(proj, q_norm_w, k_norm_w, rel_bias, bias_tiles):
    bsz, seq, _ = proj.shape
    n_blocks = seq // TILE
    assert 2 * n_blocks <= HEAD_DIM
    qw = jnp.tile(q_norm_w.astype(F32), 2).reshape(LANES, 1)
    kw = jnp.tile(k_norm_w.astype(F32), 2).reshape(1, LANES)
    tri = jnp.asarray(_suffix_matrix(), BF16)
    out_spec = pl.BlockSpec((1, TILE, WIDTH), lambda b, i: (b, i, 0))
    out_shape = jax.ShapeDtypeStruct((bsz, seq, WIDTH), BF16)
    return pl.pallas_call(
        _attn_kernel,
        grid=(bsz, n_blocks),
        in_specs=[pl.BlockSpec(memory_space=pltpu.SMEM)] + _branch_specs(seq, 0) + _branch_specs(seq, 4) + [
            pl.BlockSpec((LANES, 1), lambda b, i: (0, 0)),
            pl.BlockSpec((1, LANES), lambda b, i: (0, 0)),
            pl.BlockSpec(bias_tiles.shape, lambda b, i: (0, 0, 0, 0)),
            pl.BlockSpec(tri.shape, lambda b, i: (0, 0)),
        ],
        out_specs=[out_spec, out_spec],
        out_shape=[out_shape, out_shape],
        scratch_shapes=[
            pltpu.VMEM((N_HEADS, n_blocks, TILE, LANES), BF16),
            pltpu.VMEM((N_HEADS, n_blocks, HEAD_DIM + ONES_ROWS, TILE), BF16),
            pltpu.VMEM((n_blocks, WIDTH), F32),
            pltpu.VMEM((N_HEADS, LANES, TILE), BF16),
            pltpu.VMEM((N_HEADS, n_blocks, HEAD_DIM, TILE), BF16),
            pltpu.VMEM((N_HEADS, LANES, TILE), BF16),
            pltpu.VMEM((N_HEADS, SUBLANES, TILE), F32),
            pltpu.VMEM((N_HEADS, HEAD_DIM, TILE), F32),
        ],
        compiler_params=pltpu.CompilerParams(
            dimension_semantics=("arbitrary", "arbitrary"), vmem_limit_bytes=VMEM_LIMIT),
        name="attention",
    )(rel_bias, proj, proj, proj, proj, proj, proj, proj, proj, qw, kw, bias_tiles, tri)


def _merge_kernel(x_ref, ua_ref, ub_ref, nw_ref, gb_ref, wg_ref, wua_ref, wub_ref, wo_ref,
                  o_ref, h_ref, y_ref):
    d = x_ref.shape[1]
    h_ref[...] = _rms_rows(x_ref[...], nw_ref[...]).astype(BF16)
    for c in range(0, d, COL_CHUNK):
        cols = slice(c, c + COL_CHUNK)
        cols_b = slice(d + c, d + c + COL_CHUNK)
        ga = _dot(h_ref[...], wg_ref[:, cols]) + gb_ref[:, cols]
        gb = _dot(h_ref[...], wg_ref[:, cols_b]) + gb_ref[:, cols_b]
        ya = _dot(ua_ref[...], wua_ref[:, cols])
        yb = _dot(ub_ref[...], wub_ref[:, cols])
        y = ya / (1.0 + jnp.exp(-ga)) + yb / (1.0 + jnp.exp(-gb))
        y_ref[:, cols] = y.astype(BF16)
    for c in range(0, d, COL_CHUNK):
        cols = slice(c, c + COL_CHUNK)
        o_ref[:, cols] = x_ref[:, cols] + _dot(y_ref[...], wo_ref[:, cols])


def _merge(x2d, ua, ub, norm_w, gate_b, w_gate, w_up_a, w_up_b, w_out):
    m, d = x2d.shape
    row_spec = lambda cols: pl.BlockSpec((PROJ_ROWS, cols), lambda i: (i, 0))
    full = lambda a: pl.BlockSpec(a.shape, lambda i: (0, 0))
    return pl.pallas_call(
        _merge_kernel,
        grid=(m // PROJ_ROWS,),
        in_specs=[row_spec(d), row_spec(WIDTH), row_spec(WIDTH), full(norm_w), full(gate_b),
                  full(w_gate), full(w_up_a), full(w_up_b), full(w_out)],
        out_specs=row_spec(d),
        out_shape=jax.ShapeDtypeStruct((m, d), F32),
        scratch_shapes=[pltpu.VMEM((PROJ_ROWS, d), BF16), pltpu.VMEM((PROJ_ROWS, d), BF16)],
        compiler_params=pltpu.CompilerParams(
            dimension_semantics=("arbitrary",), vmem_limit_bytes=VMEM_LIMIT),
        name="merge",
    )(x2d, ua, ub, norm_w, gate_b, w_gate, w_up_a, w_up_b, w_out)


def kernel(x, norm_w, w_in, merge_gate_b, q_norm_w, k_norm_w, rel_bias, w_up_moba, w_up_sb, w_out):
    bsz, seq, d = x.shape
    assert seq % TILE == 0 and (bsz * seq) % PROJ_ROWS == 0 and d % COL_CHUNK == 0
    assert w_in.shape[2] == ATTN_COLS + 2 * d
    rel_bias = rel_bias.astype(F32)
    bias_tiles = _bias_tiles(rel_bias)
    x2d = x.reshape(bsz * seq, d)
    for l in range(norm_w.shape[0]):
        nw = norm_w[l].reshape(1, d).astype(F32)
        w_l = w_in[l].astype(BF16)
        proj = _project(x2d, nw, w_l[:, :ATTN_COLS]).reshape(bsz, seq, ATTN_COLS)
        ua, ub = _attention(proj, q_norm_w[l], k_norm_w[l], rel_bias, bias_tiles)
        x2d = _merge(x2d, ua.reshape(bsz * seq, WIDTH), ub.reshape(bsz * seq, WIDTH), nw,
                     merge_gate_b[l].reshape(1, 2 * d).astype(F32),
                     w_l[:, ATTN_COLS:], w_up_moba[l].astype(BF16), w_up_sb[l].astype(BF16),
                     w_out[l].astype(BF16))
    return x2d.reshape(bsz, seq, d)
```

```python
import functools
import math

import numpy as np
import jax
import jax.numpy as jnp
from jax import lax
from jax.experimental import pallas as pl
from jax.experimental.pallas import tpu as pltpu

F32 = jnp.float32
BF16 = jnp.bfloat16

HEAD_DIM = 64
N_HEADS = 8
WIDTH = N_HEADS * HEAD_DIM
MOBA_BLOCK = 256
MOBA_TOPK = 3
REL_BUCKETS = 32
REL_MAX_DIST = 128
NORM_EPS = 1e-6
NEG_INF = -1e30
SCALE = HEAD_DIM ** -0.5
LOG2_E = math.log2(math.e)
EXP2_UNDERFLOW = -150.0

LANES = 128
SUBLANES = 8
TILE = MOBA_BLOCK
Q_TILES = 2
PAIRS = WIDTH // LANES
ONES_ROWS = 16
ATTN_COLS = 8 * WIDTH
VMEM_LIMIT = 48 * 1024 * 1024
PROJ_ROWS = 512
COL_CHUNK = 512


def _dot(a, b):
    return jnp.dot(a, b, preferred_element_type=F32)


def _split_bf16(x):
    hi = x.astype(BF16)
    lo = (x - hi.astype(F32)).astype(BF16)
    return hi, lo


def _rms_rows(x, w):
    ms = jnp.mean(x * x, axis=-1, keepdims=True)
    return x * lax.rsqrt(ms + NORM_EPS) * w


def _proj_kernel(x_ref, nw_ref, w_ref, o_ref, h_ref):
    h_ref[...] = _rms_rows(x_ref[...], nw_ref[...]).astype(BF16)
    for c in range(0, ATTN_COLS, COL_CHUNK):
        o_ref[:, c:c + COL_CHUNK] = _dot(h_ref[...], w_ref[:, c:c + COL_CHUNK]).astype(BF16)


def _project(x2d, norm_w, w_in):
    m, d = x2d.shape
    return pl.pallas_call(
        _proj_kernel,
        grid=(m // PROJ_ROWS,),
        in_specs=[
            pl.BlockSpec((PROJ_ROWS, d), lambda i: (i, 0)),
            pl.BlockSpec((1, d), lambda i: (0, 0)),
            pl.BlockSpec((d, ATTN_COLS), lambda i: (0, 0)),
        ],
        out_specs=pl.BlockSpec((PROJ_ROWS, ATTN_COLS), lambda i: (i, 0)),
        out_shape=jax.ShapeDtypeStruct((m, ATTN_COLS), BF16),
        scratch_shapes=[pltpu.VMEM((PROJ_ROWS, d), BF16)],
        compiler_params=pltpu.CompilerParams(
            dimension_semantics=("arbitrary",), vmem_limit_bytes=VMEM_LIMIT),
        name="proj",
    )(x2d, norm_w, w_in)


def _t5_bucket_np(dist):
    n = np.maximum(dist, 0)
    max_exact = REL_BUCKETS // 2
    nf = np.maximum(n, 1).astype(np.float64)
    large = max_exact + (np.log(nf / max_exact) / math.log(REL_MAX_DIST / max_exact)
                         * (REL_BUCKETS - max_exact)).astype(np.int32)
    large = np.minimum(large, REL_BUCKETS - 1)
    return np.where(n < max_exact, n, large).astype(np.int32)


def _bucket_tiles():
    key = np.arange(TILE)[:, None]
    qry = np.arange(TILE)[None, :]
    own = np.where(qry >= key, _t5_bucket_np(qry - key), REL_BUCKETS)
    prev = _t5_bucket_np(TILE + qry - key)
    assert (_t5_bucket_np(2 * TILE + qry - key) == REL_BUCKETS - 1).all()
    return np.stack([own, prev]).astype(np.int32)


def _bias_kernel(rel_ref, bkt_ref, o_ref):
    h = pl.program_id(0)
    for t in range(2):
        bkt = bkt_ref[t]
        acc = jnp.full((TILE, TILE), NEG_INF, F32)
        for b in range(REL_BUCKETS):
            acc = jnp.where(bkt == b, rel_ref[h, b] * LOG2_E, acc)
        o_ref[0, t] = acc


def _bias_tiles(rel_bias):
    return pl.pallas_call(
        _bias_kernel,
        grid=(N_HEADS,),
        in_specs=[
            pl.BlockSpec(memory_space=pltpu.SMEM),
            pl.BlockSpec((2, TILE, TILE), lambda h: (0, 0, 0)),
        ],
        out_specs=pl.BlockSpec((1, 2, TILE, TILE), lambda h: (h, 0, 0, 0)),
        out_shape=jax.ShapeDtypeStruct((N_HEADS, 2, TILE, TILE), F32),
        compiler_params=pltpu.CompilerParams(dimension_semantics=("arbitrary",)),
        name="bias_tiles",
    )(rel_bias, jnp.asarray(_bucket_tiles()))


def _gated_store(o_ref, z_ref, out_t, sub):
    rows = slice(sub * TILE, (sub + 1) * TILE)
    z = z_ref[0, rows, :].astype(F32)
    gate = z / (1.0 + jnp.exp(-z))
    for p in range(PAIRS):
        lanes = slice(p * LANES, (p + 1) * LANES)
        pair_t = jnp.concatenate([out_t[2 * p], out_t[2 * p + 1]], axis=0)
        o_ref[0, rows, lanes] = (pair_t.T * gate[:, lanes]).astype(BF16)


def _interleave(*stage_lists):
    todo = [list(s) for s in stage_lists]
    done = [0] * len(todo)
    while any(done[k] < len(todo[k]) for k in range(len(todo))):
        k = min((k for k in range(len(todo)) if done[k] < len(todo[k])),
                key=lambda k: (done[k] + 0.5) / len(todo[k]))
        todo[k][done[k]]()
        done[k] += 1


def _moba_kernel(rel_ref, q_ref, k_ref, v_ref, z_ref, qw_ref, kw_ref, bias_ref, o_ref,
                 kaug_ref, vt_ref, kmean_ref, qaug_ref):
    step = pl.program_id(1)
    n_blocks = k_ref.shape[1] // TILE

    @pl.when(step == 0)
    def _prepare_keys_values():
        lane = lax.broadcasted_iota(jnp.int32, (TILE, LANES), 1)
        head0 = lane < HEAD_DIM
        ones = jnp.ones((ONES_ROWS, TILE), BF16)
        for blk in range(n_blocks):
            rows = slice(blk * TILE, (blk + 1) * TILE)
            hot0 = ((lane == HEAD_DIM + blk) | (lane == HEAD_DIM + SUBLANES + blk)).astype(F32)
            hot1 = ((lane == blk) | (lane == SUBLANES + blk)).astype(F32)
            for p in range(PAIRS):
                lanes = slice(p * LANES, (p + 1) * LANES)
                kb = k_ref[0, rows, lanes].astype(F32)
                sq = kb * kb
                ss_all = jnp.sum(sq, axis=-1, keepdims=True)
                ss0 = jnp.sum(jnp.where(head0, sq, 0.0), axis=-1, keepdims=True)
                r0 = lax.rsqrt(ss0 * (1.0 / HEAD_DIM) + NORM_EPS)
                r1 = lax.rsqrt((ss_all - ss0) * (1.0 / HEAD_DIM) + NORM_EPS)
                khat = kb * jnp.where(head0, r0, r1) * kw_ref[...]
                kmean_ref[blk:blk + 1, lanes] = jnp.mean(khat, axis=0, keepdims=True)
                kaug_ref[2 * p, blk] = jnp.where(head0, khat, hot0).astype(BF16)
                kaug_ref[2 * p + 1, blk] = jnp.where(head0, hot1, khat).astype(BF16)
                vt = v_ref[0, rows, lanes].astype(F32).T
                vt_ref[2 * p, blk, 0:HEAD_DIM, :] = vt[0:HEAD_DIM].astype(BF16)
                vt_ref[2 * p, blk, HEAD_DIM:, :] = ones
                vt_ref[2 * p + 1, blk, 0:HEAD_DIM, :] = vt[HEAD_DIM:].astype(BF16)
                vt_ref[2 * p + 1, blk, HEAD_DIM:, :] = ones

    row = lax.broadcasted_iota(jnp.int32, (LANES, TILE), 0)
    top = row < HEAD_DIM
    lane8 = lax.broadcasted_iota(jnp.int32, (n_blocks, LANES), 1)
    units = [(sub, p) for sub in range(Q_TILES) for p in range(PAIRS)]
    qts = [q_ref[0, sub * TILE:(sub + 1) * TILE, p * LANES:(p + 1) * LANES].astype(F32).T
           for sub, p in units]
    qhats = []
    for qt in qts:
        sq = qt * qt
        r0 = lax.rsqrt(jnp.sum(sq[0:HEAD_DIM], axis=0, keepdims=True) * (1.0 / HEAD_DIM) + NORM_EPS)
        r1 = lax.rsqrt(jnp.sum(sq[HEAD_DIM:], axis=0, keepdims=True) * (1.0 / HEAD_DIM) + NORM_EPS)
        qhats.append(qt * jnp.where(top, r0, r1) * qw_ref[...])
    gates = []
    for (sub, p), qhat in zip(units, qhats):
        km = kmean_ref[:, p * LANES:(p + 1) * LANES]
        km2 = jnp.concatenate([jnp.where(lane8 < HEAD_DIM, km, 0.0),
                               jnp.where(lane8 < HEAD_DIM, 0.0, km)], axis=0)
        kh, kl = _split_bf16(km2)
        qh, ql = _split_bf16(qhat)
        gates.append(_dot(kh, qh) + _dot(kh, ql) + _dot(kl, qh))

    def block_terms(g, far_bias, i):
        n = lax.broadcasted_iota(jnp.int32, g.shape, 0)
        cnt = jnp.zeros(g.shape, jnp.int32)
        for m in range(n_blocks):
            gm = g[m:m + 1, :]
            beats = (gm > g) | ((gm == g) & (m < n))
            cnt = cnt + jnp.where(beats, (m < i).astype(jnp.int32), 0)
        keep = ((cnt < MOBA_TOPK) & (n < i)) | (n == i)
        hi, lo = _split_bf16(jnp.where(keep, jnp.where(n < i - 1, far_bias, 0.0), NEG_INF))
        return [hi.astype(F32), lo.astype(F32)]

    pad = jnp.zeros((HEAD_DIM - 2 * n_blocks, TILE), F32)
    for (sub, p), qhat, gate in zip(units, qhats, gates):
        i = step * Q_TILES + sub
        qs = qhat * (SCALE * LOG2_E)
        far0 = rel_ref[2 * p, REL_BUCKETS - 1] * LOG2_E
        far1 = rel_ref[2 * p + 1, REL_BUCKETS - 1] * LOG2_E
        qaug_ref[sub, 2 * p] = jnp.concatenate(
            [qs[0:HEAD_DIM]] + block_terms(gate[0:n_blocks], far0, i) + [pad], axis=0).astype(BF16)
        qaug_ref[sub, 2 * p + 1] = jnp.concatenate(
            block_terms(gate[n_blocks:], far1, i) + [pad, qs[HEAD_DIM:]], axis=0).astype(BF16)

    def attend_stages(n_past, sub):
        s, p, out = {}, {}, {}

        def stage(t):
            if t < N_HEADS:
                qa = qaug_ref[sub, t]
                s[t] = []
                for j in range(n_past + 1):
                    sj = _dot(kaug_ref[t, j], qa)
                    if n_past - j < 2:
                        sj = sj + bias_ref[t, n_past - j]
                    s[t].append(sj)
            if t >= 2:
                h = t - 2
                acc = None
                for j, pj in enumerate(p.pop(h)):
                    pv = _dot(vt_ref[h, j], pj)
                    acc = pv if acc is None else acc + pv
                out[h] = acc[0:HEAD_DIM] * (1.0 / acc[HEAD_DIM:HEAD_DIM + 1])
            if 1 <= t <= N_HEADS:
                h = t - 1
                m = functools.reduce(jnp.maximum, [jnp.max(sj, axis=0, keepdims=True) for sj in s[h]])
                p[h] = [jnp.exp2(sj - m).astype(BF16) for sj in s.pop(h)]
            if t == N_HEADS + 1:
                _gated_store(o_ref, z_ref, out, sub)

        return [functools.partial(stage, t) for t in range(N_HEADS + 2)]

    def attend_group(g):
        _interleave(*[attend_stages(g * Q_TILES + sub, sub) for sub in range(Q_TILES)])

    for g in range(n_blocks // Q_TILES):
        pl.when(step == g)(functools.partial(attend_group, g))


def _branch_specs(seq, first):
    return [
        pl.BlockSpec((1, Q_TILES * TILE, WIDTH), lambda b, i: (b, i, first)),
        pl.BlockSpec((1, seq, WIDTH), lambda b, i: (b, 0, first + 1)),
        pl.BlockSpec((1, seq, WIDTH), lambda b, i: (b, 0, first + 2)),
        pl.BlockSpec((1, Q_TILES * TILE, WIDTH), lambda b, i: (b, i, first + 3)),
    ]


def _moba(proj, q_norm_w, k_norm_w, rel_bias, bias_tiles):
    bsz, seq, _ = proj.shape
    n_blocks = seq // TILE
    assert 2 * n_blocks <= HEAD_DIM and n_blocks % Q_TILES == 0
    qw = jnp.tile(q_norm_w.astype(F32), 2).reshape(LANES, 1)
    kw = jnp.tile(k_norm_w.astype(F32), 2).reshape(1, LANES)
    return pl.pallas_call(
        _moba_kernel,
        grid=(bsz, n_blocks // Q_TILES),
        in_specs=[pl.BlockSpec(memory_space=pltpu.SMEM)] + _branch_specs(seq, 0) + [
            pl.BlockSpec((LANES, 1), lambda b, i: (0, 0)),
            pl.BlockSpec((1, LANES), lambda b, i: (0, 0)),
            pl.BlockSpec(bias_tiles.shape, lambda b, i: (0, 0, 0, 0)),
        ],
        out_specs=pl.BlockSpec((1, Q_TILES * TILE, WIDTH), lambda b, i: (b, i, 0)),
        out_shape=jax.ShapeDtypeStruct((bsz, seq, WIDTH), BF16),
        scratch_shapes=[
            pltpu.VMEM((N_HEADS, n_blocks, TILE, LANES), BF16),
            pltpu.VMEM((N_HEADS, n_blocks, HEAD_DIM + ONES_ROWS, TILE), BF16),
            pltpu.VMEM((n_blocks, WIDTH), F32),
            pltpu.VMEM((Q_TILES, N_HEADS, LANES, TILE), BF16),
        ],
        compiler_params=pltpu.CompilerParams(
            dimension_semantics=("arbitrary", "arbitrary"), vmem_limit_bytes=VMEM_LIMIT),
        name="moba",
    )(rel_bias, proj, proj, proj, proj, qw, kw, bias_tiles)


def _suffix_matrix():
    s = np.arange(TILE)[:, None]
    j = np.arange(TILE)[None, :]
    return np.concatenate([np.where(j > s, -1.0, 0.0), -np.ones((ONES_ROWS, TILE))]).astype(np.float32)


def _sb_kernel(q_ref, k_ref, v_ref, z_ref, tri_ref, o_ref, vt_ref, qs_ref):
    step = pl.program_id(1)
    n_blocks = k_ref.shape[1] // TILE
    heads = range(N_HEADS)

    @pl.when(step == 0)
    def _transpose_values():
        for blk in range(n_blocks):
            for p in range(PAIRS):
                vt = v_ref[0, blk * TILE:(blk + 1) * TILE, p * LANES:(p + 1) * LANES].astype(F32).T
                vt_ref[2 * p, blk] = vt[0:HEAD_DIM].astype(BF16)
                vt_ref[2 * p + 1, blk] = vt[HEAD_DIM:].astype(BF16)

    row = lax.broadcasted_iota(jnp.int32, (LANES, TILE), 0)
    for sub in range(Q_TILES):
        for p in range(PAIRS):
            qt = q_ref[0, sub * TILE:(sub + 1) * TILE, p * LANES:(p + 1) * LANES].astype(F32).T
            qt = qt * (SCALE * LOG2_E)
            qs_ref[sub, 2 * p] = jnp.where(row < HEAD_DIM, qt, 0.0).astype(BF16)
            qs_ref[sub, 2 * p + 1] = jnp.where(row < HEAD_DIM, 0.0, qt).astype(BF16)
    key = lax.broadcasted_iota(jnp.int32, (TILE, TILE), 0)
    qry = lax.broadcasted_iota(jnp.int32, (TILE, TILE), 1)
    past = key < qry

    def scores(j, h, sub):
        lanes = slice((h // 2) * LANES, (h // 2 + 1) * LANES)
        kb = k_ref[0, pl.ds(pl.multiple_of(j * TILE, TILE), TILE), lanes]
        return _dot(kb, qs_ref[sub, h])

    def softplus2(z2):
        neg_abs = lax.bitcast_convert_type(
            lax.bitcast_convert_type(z2, jnp.uint32) | jnp.uint32(0x80000000), F32)
        return jnp.maximum(z2, 0.0) + jnp.log2(1.0 + jnp.exp2(neg_abs))

    def suffix(nl):
        return _dot(tri_ref[...], nl.astype(BF16))

    first = step * Q_TILES
    prev_off = jnp.where(first > 0, 0.0, -jnp.inf)
    chains = [(sub, h, own) for sub in range(Q_TILES) for h in heads for own in (True, False)]
    n = len(chains)

    def tile_of(c):
        sub, _, own = chains[c]
        return first + sub if own else jnp.maximum(first + sub - 1, 0)

    zs, nls, lw, sufs, ws, pvs = {}, {}, {}, {}, {}, {}
    for t in range(n + 4):
        if t < n:
            zs[t] = scores(tile_of(t), chains[t][1], chains[t][0])
        if 0 <= t - 2 < n:
            sufs[t - 2] = suffix(nls.pop(t - 2))
        if 0 <= t - 4 < n:
            c = t - 4
            pvs[c] = _dot(vt_ref[chains[c][1], tile_of(c)], ws.pop(c))
        if 0 <= t - 1 < n:
            c = t - 1
            nl = softplus2(zs[c])
            if chains[c][2]:
                nl = jnp.where(past, nl, 0.0)
            lw[c] = zs.pop(c) - nl
            nls[c] = nl
        if 0 <= t - 3 < n:
            c = t - 3
            if chains[c][2]:
                w = jnp.where(past, jnp.exp2(lw.pop(c) + sufs[c][0:TILE]), 0.0)
            else:
                off = sufs[c - 1][TILE:TILE + 1]
                if chains[c][0] == 0:
                    off = off + prev_off
                w = jnp.exp2(lw.pop(c) + sufs[c][0:TILE] + off)
            ws[c] = w.astype(BF16)

    def alive(laters):
        top = functools.reduce(jnp.maximum, laters)
        return (jnp.max(top) > EXP2_UNDERFLOW).astype(jnp.int32)

    def far_cond(carry):
        return (carry[0] >= 0) & (carry[1] > 0)

    for sub in range(Q_TILES):
        base = sub * 2 * N_HEADS
        later = [sufs[base + 2 * h][TILE:TILE + 1] + sufs[base + 2 * h + 1][TILE:TILE + 1] for h in heads]
        acc = [pvs[base + 2 * h] + pvs[base + 2 * h + 1] for h in heads]

        def far_tile(carry, sub=sub):
            j, _, later, acc = carry
            new_later, new_acc = [], []
            for h in heads:
                z = scores(j, h, sub)
                nl = softplus2(z)
                suf = suffix(nl)
                w = jnp.exp2((z - nl) + suf[0:TILE] + later[h])
                new_later.append(later[h] + suf[TILE:TILE + 1])
                new_acc.append(acc[h] + _dot(vt_ref[h, j], w.astype(BF16)))
            return j - 1, alive(new_later), new_later, new_acc

        _, _, _, acc = lax.while_loop(far_cond, far_tile, (first + sub - 2, alive(later), later, acc))
        _gated_store(o_ref, z_ref, acc, sub)


def _stick_breaking(proj):
    bsz, seq, _ = proj.shape
    n_blocks = seq // TILE
    assert n_blocks % Q_TILES == 0
    tri = jnp.asarray(_suffix_matrix(), BF16)
    return pl.pallas_call(
        _sb_kernel,
        grid=(bsz, n_blocks // Q_TILES),
        in_specs=_branch_specs(seq, 4) + [pl.BlockSpec(tri.shape, lambda b, i: (0, 0))],
        out_specs=pl.BlockSpec((1, Q_TILES * TILE, WIDTH), lambda b, i: (b, i, 0)),
        out_shape=jax.ShapeDtypeStruct((bsz, seq, WIDTH), BF16),
        scratch_shapes=[pltpu.VMEM((N_HEADS, n_blocks, HEAD_DIM, TILE), BF16),
                        pltpu.VMEM((Q_TILES, N_HEADS, LANES, TILE), BF16)],
        compiler_params=pltpu.CompilerParams(
            dimension_semantics=("arbitrary", "arbitrary"), vmem_limit_bytes=VMEM_LIMIT),
        name="stick_breaking",
    )(proj, proj, proj, proj, tri)


def _merge_kernel(x_ref, ua_ref, ub_ref, nw_ref, gb_ref, wg_ref, wua_ref, wub_ref, wo_ref,
                  o_ref, h_ref, y_ref):
    d = x_ref.shape[1]
    h_ref[...] = _rms_rows(x_ref[...], nw_ref[...]).astype(BF16)
    for c in range(0, d, COL_CHUNK):
        cols = slice(c, c + COL_CHUNK)
        cols_b = slice(d + c, d + c + COL_CHUNK)
        ga = _dot(h_ref[...], wg_ref[:, cols]) + gb_ref[:, cols]
        gb = _dot(h_ref[...], wg_ref[:, cols_b]) + gb_ref[:, cols_b]
        ya = _dot(ua_ref[...], wua_ref[:, cols])
        yb = _dot(ub_ref[...], wub_ref[:, cols])
        y = ya / (1.0 + jnp.exp(-ga)) + yb / (1.0 + jnp.exp(-gb))
        y_ref[:, cols] = y.astype(BF16)
    for c in range(0, d, COL_CHUNK):
        cols = slice(c, c + COL_CHUNK)
        o_ref[:, cols] = x_ref[:, cols] + _dot(y_ref[...], wo_ref[:, cols])


def _merge(x2d, ua, ub, norm_w, gate_b, w_in, w_up_a, w_up_b, w_out):
    m, d = x2d.shape
    assert ATTN_COLS % (2 * d) == 0
    row_spec = lambda cols: pl.BlockSpec((PROJ_ROWS, cols), lambda i: (i, 0))
    full = lambda a: pl.BlockSpec(a.shape, lambda i: (0, 0))
    gate_cols = pl.BlockSpec((d, 2 * d), lambda i: (0, ATTN_COLS // (2 * d)))
    return pl.pallas_call(
        _merge_kernel,
        grid=(m // PROJ_ROWS,),
        in_specs=[row_spec(d), row_spec(WIDTH), row_spec(WIDTH), full(norm_w), full(gate_b),
                  gate_cols, full(w_up_a), full(w_up_b), full(w_out)],
        out_specs=row_spec(d),
        out_shape=jax.ShapeDtypeStruct((m, d), F32),
        scratch_shapes=[pltpu.VMEM((PROJ_ROWS, d), BF16), pltpu.VMEM((PROJ_ROWS, d), BF16)],
        compiler_params=pltpu.CompilerParams(
            dimension_semantics=("arbitrary",), vmem_limit_bytes=VMEM_LIMIT),
        name="merge",
    )(x2d, ua, ub, norm_w, gate_b, w_in, w_up_a, w_up_b, w_out)


def kernel(x, norm_w, w_in, merge_gate_b, q_norm_w, k_norm_w, rel_bias, w_up_moba, w_up_sb, w_out):
    bsz, seq, d = x.shape
    assert seq % TILE == 0 and (bsz * seq) % PROJ_ROWS == 0 and d % COL_CHUNK == 0
    assert w_in.shape[2] == ATTN_COLS + 2 * d
    rel_bias = rel_bias.astype(F32)
    bias_tiles = _bias_tiles(rel_bias)
    x2d = x.reshape(bsz * seq, d)
    for l in range(norm_w.shape[0]):
        nw = norm_w[l].reshape(1, d).astype(F32)
        w_l = w_in[l].astype(BF16)
        proj = _project(x2d, nw, w_l).reshape(bsz, seq, ATTN_COLS)
        ua = _moba(proj, q_norm_w[l], k_norm_w[l], rel_bias, bias_tiles).reshape(bsz * seq, WIDTH)
        ub = _stick_breaking(proj).reshape(bsz * seq, WIDTH)
        x2d = _merge(x2d, ua, ub, nw, merge_gate_b[l].reshape(1, 2 * d).astype(F32),
                     w_l, w_up_moba[l].astype(BF16), w_up_sb[l].astype(BF16),
                     w_out[l].astype(BF16))
    return x2d.reshape(bsz, seq, d)
```

```python
import functools
import math

import numpy as np
import jax
import jax.numpy as jnp
from jax import lax
from jax.experimental import pallas as pl
from jax.experimental.pallas import tpu as pltpu

F32 = jnp.float32
BF16 = jnp.bfloat16

HEAD_DIM = 64
N_HEADS = 8
WIDTH = N_HEADS * HEAD_DIM
MOBA_BLOCK = 256
MOBA_TOPK = 3
REL_BUCKETS = 32
REL_MAX_DIST = 128
NORM_EPS = 1e-6
NEG_INF = -1e30
SCALE = HEAD_DIM ** -0.5
LOG2_E = math.log2(math.e)
EXP2_UNDERFLOW = -150.0
EXP2_CLAMP = 126.0

LANES = 128
SUBLANES = 8
TILE = MOBA_BLOCK
PAIRS = WIDTH // LANES
ONES_ROWS = 16
ATTN_COLS = 8 * WIDTH
VMEM_LIMIT = 56 * 1024 * 1024
PROJ_ROWS = 512
COL_CHUNK = 512


def _dot(a, b):
    return jnp.dot(a, b, preferred_element_type=F32)


def _split_bf16(x):
    hi = x.astype(BF16)
    lo = (x - hi.astype(F32)).astype(BF16)
    return hi, lo


def _rms_rows(x, w):
    ms = jnp.mean(x * x, axis=-1, keepdims=True)
    return x * lax.rsqrt(ms + NORM_EPS) * w


def _proj_kernel(x_ref, nw_ref, w_ref, o_ref, h_ref):
    h_ref[...] = _rms_rows(x_ref[...], nw_ref[...]).astype(BF16)
    for c in range(0, ATTN_COLS, COL_CHUNK):
        o_ref[:, c:c + COL_CHUNK] = _dot(h_ref[...], w_ref[:, c:c + COL_CHUNK]).astype(BF16)


def _project(x2d, norm_w, w_in):
    m, d = x2d.shape
    return pl.pallas_call(
        _proj_kernel,
        grid=(m // PROJ_ROWS,),
        in_specs=[
            pl.BlockSpec((PROJ_ROWS, d), lambda i: (i, 0)),
            pl.BlockSpec((1, d), lambda i: (0, 0)),
            pl.BlockSpec((d, ATTN_COLS), lambda i: (0, 0)),
        ],
        out_specs=pl.BlockSpec((PROJ_ROWS, ATTN_COLS), lambda i: (i, 0)),
        out_shape=jax.ShapeDtypeStruct((m, ATTN_COLS), BF16),
        scratch_shapes=[pltpu.VMEM((PROJ_ROWS, d), BF16)],
        compiler_params=pltpu.CompilerParams(
            dimension_semantics=("arbitrary",), vmem_limit_bytes=VMEM_LIMIT),
        name="proj",
    )(x2d, norm_w, w_in)


def _t5_bucket_np(dist):
    n = np.maximum(dist, 0)
    max_exact = REL_BUCKETS // 2
    nf = np.maximum(n, 1).astype(np.float64)
    large = max_exact + (np.log(nf / max_exact) / math.log(REL_MAX_DIST / max_exact)
                         * (REL_BUCKETS - max_exact)).astype(np.int32)
    large = np.minimum(large, REL_BUCKETS - 1)
    return np.where(n < max_exact, n, large).astype(np.int32)


def _bucket_tiles():
    key = np.arange(TILE)[:, None]
    qry = np.arange(TILE)[None, :]
    own = np.where(qry >= key, _t5_bucket_np(qry - key), REL_BUCKETS)
    prev = _t5_bucket_np(TILE + qry - key)
    assert (_t5_bucket_np(2 * TILE + qry - key) == REL_BUCKETS - 1).all()
    return np.stack([own, prev]).astype(np.int32)


def _bias_kernel(rel_ref, bkt_ref, o_ref):
    h = pl.program_id(0)
    for t in range(2):
        bkt = bkt_ref[t]
        acc = jnp.full((TILE, TILE), NEG_INF, F32)
        for b in range(REL_BUCKETS):
            acc = jnp.where(bkt == b, rel_ref[h, b] * LOG2_E, acc)
        o_ref[0, t] = acc


def _bias_tiles(rel_bias):
    return pl.pallas_call(
        _bias_kernel,
        grid=(N_HEADS,),
        in_specs=[
            pl.BlockSpec(memory_space=pltpu.SMEM),
            pl.BlockSpec((2, TILE, TILE), lambda h: (0, 0, 0)),
        ],
        out_specs=pl.BlockSpec((1, 2, TILE, TILE), lambda h: (h, 0, 0, 0)),
        out_shape=jax.ShapeDtypeStruct((N_HEADS, 2, TILE, TILE), F32),
        compiler_params=pltpu.CompilerParams(dimension_semantics=("arbitrary",)),
        name="bias_tiles",
    )(rel_bias, jnp.asarray(_bucket_tiles()))


def _gated_store(o_ref, z_ref, out_t, tile):
    rows = pl.ds(pl.multiple_of(tile * TILE, TILE), TILE)
    z = z_ref[0, rows, :].astype(F32)
    gate = z / (1.0 + jnp.exp(-z))
    for p in range(PAIRS):
        lanes = slice(p * LANES, (p + 1) * LANES)
        pair_t = jnp.concatenate([out_t[2 * p], out_t[2 * p + 1]], axis=0)
        o_ref[0, rows, lanes] = (pair_t.T * gate[:, lanes]).astype(BF16)


def _interleave(*stage_lists):
    todo = [list(s) for s in stage_lists]
    done = [0] * len(todo)
    while any(done[k] < len(todo[k]) for k in range(len(todo))):
        k = min((k for k in range(len(todo)) if done[k] < len(todo[k])),
                key=lambda k: (done[k] + 0.5) / len(todo[k]))
        todo[k][done[k]]()
        done[k] += 1


def _moba_kernel(rel_ref, q_ref, k_ref, v_ref, z_ref, qw_ref, kw_ref, bias_ref, o_ref,
                 kaug_ref, vt_ref, kmean_ref, qaug_ref):
    n_blocks = k_ref.shape[1] // TILE
    pairs = range(PAIRS)

    lane = lax.broadcasted_iota(jnp.int32, (TILE, LANES), 1)
    head0 = lane < HEAD_DIM
    ones = jnp.ones((ONES_ROWS, TILE), BF16)
    for blk in range(n_blocks):
        rows = slice(blk * TILE, (blk + 1) * TILE)
        hot0 = ((lane == HEAD_DIM + blk) | (lane == HEAD_DIM + SUBLANES + blk)).astype(F32)
        hot1 = ((lane == blk) | (lane == SUBLANES + blk)).astype(F32)
        for p in pairs:
            lanes = slice(p * LANES, (p + 1) * LANES)
            kb = k_ref[0, rows, lanes].astype(F32)
            sq = kb * kb
            ss_all = jnp.sum(sq, axis=-1, keepdims=True)
            ss0 = jnp.sum(jnp.where(head0, sq, 0.0), axis=-1, keepdims=True)
            r0 = lax.rsqrt(ss0 * (1.0 / HEAD_DIM) + NORM_EPS)
            r1 = lax.rsqrt((ss_all - ss0) * (1.0 / HEAD_DIM) + NORM_EPS)
            khat = kb * jnp.where(head0, r0, r1) * kw_ref[...]
            kmean_ref[blk:blk + 1, lanes] = jnp.mean(khat, axis=0, keepdims=True)
            kaug_ref[2 * p, blk] = jnp.where(head0, khat, hot0).astype(BF16)
            kaug_ref[2 * p + 1, blk] = jnp.where(head0, hot1, khat).astype(BF16)
            vt = v_ref[0, rows, lanes].astype(F32).T
            vt_ref[2 * p, blk, 0:HEAD_DIM, :] = vt[0:HEAD_DIM].astype(BF16)
            vt_ref[2 * p, blk, HEAD_DIM:, :] = ones
            vt_ref[2 * p + 1, blk, 0:HEAD_DIM, :] = vt[HEAD_DIM:].astype(BF16)
            vt_ref[2 * p + 1, blk, HEAD_DIM:, :] = ones

    row = lax.broadcasted_iota(jnp.int32, (LANES, TILE), 0)
    top = row < HEAD_DIM
    lane8 = lax.broadcasted_iota(jnp.int32, (n_blocks, LANES), 1)
    units = [(i, p) for i in range(n_blocks) for p in pairs]
    qts = [q_ref[0, i * TILE:(i + 1) * TILE, p * LANES:(p + 1) * LANES].astype(F32).T
           for i, p in units]
    qhats = []
    for qt in qts:
        sq = qt * qt
        r0 = lax.rsqrt(jnp.sum(sq[0:HEAD_DIM], axis=0, keepdims=True) * (1.0 / HEAD_DIM) + NORM_EPS)
        r1 = lax.rsqrt(jnp.sum(sq[HEAD_DIM:], axis=0, keepdims=True) * (1.0 / HEAD_DIM) + NORM_EPS)
        qhats.append(qt * jnp.where(top, r0, r1) * qw_ref[...])
    gates = []
    for (i, p), qhat in zip(units, qhats):
        km = kmean_ref[:, p * LANES:(p + 1) * LANES]
        km2 = jnp.concatenate([jnp.where(lane8 < HEAD_DIM, km, 0.0),
                               jnp.where(lane8 < HEAD_DIM, 0.0, km)], axis=0)
        kh, kl = _split_bf16(km2)
        qh, ql = _split_bf16(qhat)
        gates.append(_dot(kh, qh) + _dot(kh, ql) + _dot(kl, qh))

    def block_terms(g, far_bias, i):
        n = lax.broadcasted_iota(jnp.int32, g.shape, 0)
        cnt = jnp.zeros(g.shape, jnp.int32)
        for m in range(i):
            gm = g[m:m + 1, :]
            beats = (gm > g) | ((gm == g) & (m < n))
            cnt = cnt + jnp.where(beats, 1, 0)
        keep = ((cnt < MOBA_TOPK) & (n < i)) | (n == i)
        hi, lo = _split_bf16(jnp.where(keep, jnp.where(n < i - 1, far_bias, 0.0), NEG_INF))
        return [hi.astype(F32), lo.astype(F32)]

    pad = jnp.zeros((HEAD_DIM - 2 * n_blocks, TILE), F32)
    for (i, p), qhat, gate in zip(units, qhats, gates):
        qs = qhat * (SCALE * LOG2_E)
        far0 = rel_ref[2 * p, REL_BUCKETS - 1] * LOG2_E
        far1 = rel_ref[2 * p + 1, REL_BUCKETS - 1] * LOG2_E
        qaug_ref[i, 2 * p] = jnp.concatenate(
            [qs[0:HEAD_DIM]] + block_terms(gate[0:n_blocks], far0, i) + [pad], axis=0).astype(BF16)
        qaug_ref[i, 2 * p + 1] = jnp.concatenate(
            block_terms(gate[n_blocks:], far1, i) + [pad, qs[HEAD_DIM:]], axis=0).astype(BF16)

    def attend_stages(i):
        s, p, out = {}, {}, {}

        def stage(t):
            if t < N_HEADS:
                qa = qaug_ref[i, t]
                s[t] = []
                for j in range(i + 1):
                    sj = _dot(kaug_ref[t, j], qa)
                    if i - j < 2:
                        sj = sj + bias_ref[t, i - j]
                    s[t].append(sj)
            if t >= 2:
                h = t - 2
                acc = None
                for j, pj in enumerate(p.pop(h)):
                    pv = _dot(vt_ref[h, j], pj)
                    acc = pv if acc is None else acc + pv
                out[h] = acc[0:HEAD_DIM] * (1.0 / acc[HEAD_DIM:HEAD_DIM + 1])
            if 1 <= t <= N_HEADS:
                h = t - 1
                m = functools.reduce(jnp.maximum, [jnp.max(sj, axis=0, keepdims=True) for sj in s[h]])
                p[h] = [jnp.exp2(sj - m).astype(BF16) for sj in s.pop(h)]
            if t == N_HEADS + 1:
                _gated_store(o_ref, z_ref, out, i)

        return [functools.partial(stage, t) for t in range(N_HEADS + 2)]

    _interleave(*[attend_stages(i) for i in range(n_blocks)])


def _branch_specs(seq, first):
    return [pl.BlockSpec((1, seq, WIDTH), functools.partial(lambda c, b: (b, 0, c), first + c))
            for c in range(4)]


def _moba(proj, q_norm_w, k_norm_w, rel_bias, bias_tiles):
    bsz, seq, _ = proj.shape
    n_blocks = seq // TILE
    assert 2 * n_blocks <= HEAD_DIM
    qw = jnp.tile(q_norm_w.astype(F32), 2).reshape(LANES, 1)
    kw = jnp.tile(k_norm_w.astype(F32), 2).reshape(1, LANES)
    return pl.pallas_call(
        _moba_kernel,
        grid=(bsz,),
        in_specs=[pl.BlockSpec(memory_space=pltpu.SMEM)] + _branch_specs(seq, 0) + [
            pl.BlockSpec((LANES, 1), lambda b: (0, 0)),
            pl.BlockSpec((1, LANES), lambda b: (0, 0)),
            pl.BlockSpec(bias_tiles.shape, lambda b: (0, 0, 0, 0)),
        ],
        out_specs=pl.BlockSpec((1, seq, WIDTH), lambda b: (b, 0, 0)),
        out_shape=jax.ShapeDtypeStruct((bsz, seq, WIDTH), BF16),
        scratch_shapes=[
            pltpu.VMEM((N_HEADS, n_blocks, TILE, LANES), BF16),
            pltpu.VMEM((N_HEADS, n_blocks, HEAD_DIM + ONES_ROWS, TILE), BF16),
            pltpu.VMEM((n_blocks, WIDTH), F32),
            pltpu.VMEM((n_blocks, N_HEADS, LANES, TILE), BF16),
        ],
        compiler_params=pltpu.CompilerParams(
            dimension_semantics=("arbitrary",), vmem_limit_bytes=VMEM_LIMIT),
        name="moba",
    )(rel_bias, proj, proj, proj, proj, qw, kw, bias_tiles)


def _suffix_matrix():
    s = np.arange(TILE)[:, None]
    j = np.arange(TILE)[None, :]
    return np.concatenate([np.where(j > s, -1.0, 0.0), -np.ones((ONES_ROWS, TILE))]).astype(np.float32)


def _sb_kernel(q_ref, k_ref, v_ref, z_ref, tri_ref, o_ref,
               vt_ref, qs_ref, later_ref, acc_ref, alive_ref):
    n_blocks = k_ref.shape[1] // TILE
    heads = range(N_HEADS)

    for blk in range(n_blocks):
        for p in range(PAIRS):
            vt = v_ref[0, blk * TILE:(blk + 1) * TILE, p * LANES:(p + 1) * LANES].astype(F32).T
            vt_ref[2 * p, blk] = vt[0:HEAD_DIM].astype(BF16)
            vt_ref[2 * p + 1, blk] = vt[HEAD_DIM:].astype(BF16)

    row = lax.broadcasted_iota(jnp.int32, (LANES, TILE), 0)
    for i in range(n_blocks):
        for p in range(PAIRS):
            qt = q_ref[0, i * TILE:(i + 1) * TILE, p * LANES:(p + 1) * LANES].astype(F32).T
            qt = qt * (SCALE * LOG2_E)
            qs_ref[i, 2 * p] = jnp.where(row < HEAD_DIM, qt, 0.0).astype(BF16)
            qs_ref[i, 2 * p + 1] = jnp.where(row < HEAD_DIM, 0.0, qt).astype(BF16)
    key = lax.broadcasted_iota(jnp.int32, (TILE, TILE), 0)
    qry = lax.broadcasted_iota(jnp.int32, (TILE, TILE), 1)
    past = key < qry

    def scores(j, h, i):
        lanes = slice((h // 2) * LANES, (h // 2 + 1) * LANES)
        kb = k_ref[0, pl.ds(pl.multiple_of(j * TILE, TILE), TILE), lanes]
        return _dot(kb, qs_ref[i, h])

    def softplus2(z2):
        return jnp.maximum(z2, jnp.log2(1.0 + jnp.exp2(jnp.minimum(z2, EXP2_CLAMP))))

    def suffix(nl):
        return _dot(tri_ref[...], nl.astype(BF16))

    chains = [(i, h, j) for i in range(n_blocks) for h in heads for j in range(i, max(i - 2, -1), -1)]
    n = len(chains)
    zs, nls, lw, sufs, ws, pvs = {}, {}, {}, {}, {}, {}
    for t in range(n + 4):
        if t < n:
            i, h, j = chains[t]
            zs[t] = scores(j, h, i)
        if 0 <= t - 2 < n:
            sufs[t - 2] = suffix(nls.pop(t - 2))
        if 0 <= t - 4 < n:
            c = t - 4
            i, h, j = chains[c]
            pvs[c] = _dot(vt_ref[h, j], ws.pop(c))
        if 0 <= t - 1 < n:
            c = t - 1
            i, h, j = chains[c]
            nl = softplus2(zs[c])
            if j == i:
                nl = jnp.where(past, nl, 0.0)
            lw[c] = zs.pop(c) - nl
            nls[c] = nl
        if 0 <= t - 3 < n:
            c = t - 3
            i, h, j = chains[c]
            if j == i:
                w = jnp.where(past, jnp.exp2(lw.pop(c) + sufs[c][0:TILE]), 0.0)
            else:
                w = jnp.exp2(lw.pop(c) + sufs[c][0:TILE] + sufs[c - 1][TILE:TILE + 1])
            ws[c] = w.astype(BF16)

    by_tile = {}
    for c, (i, h, j) in enumerate(chains):
        by_tile.setdefault((i, h), []).append(c)
    for i in range(n_blocks):
        later = [sum(sufs[c][TILE:TILE + SUBLANES] for c in by_tile[i, h]) for h in heads]
        acc = [sum(pvs[c] for c in by_tile[i, h]) for h in heads]
        _gated_store(o_ref, z_ref, acc, i)
        if i >= 2:
            for h in heads:
                later_ref[i, h] = later[h]
                acc_ref[i, h] = acc[h]
            top_later = functools.reduce(jnp.maximum, later)
            alive_ref[i] = (jnp.max(top_later) > EXP2_UNDERFLOW).astype(jnp.int32)

    def far_tiles(i, _):
        @pl.when(alive_ref[i] > 0)
        def _():
            def cond(carry):
                return (carry[0] >= 0) & (carry[1] > 0)

            def body(carry):
                j, _, later, acc = carry
                new_later, new_acc = [], []
                for h in heads:
                    z = scores(j, h, i)
                    nl = softplus2(z)
                    suf = suffix(nl)
                    w = jnp.exp2((z - nl) + suf[0:TILE] + later[h])
                    new_later.append(later[h] + suf[TILE:TILE + 1])
                    new_acc.append(acc[h] + _dot(vt_ref[h, j], w.astype(BF16)))
                top_later = functools.reduce(jnp.maximum, new_later)
                alive = (jnp.max(top_later) > EXP2_UNDERFLOW).astype(jnp.int32)
                return j - 1, alive, new_later, new_acc

            later = [later_ref[i, h][0:1] for h in heads]
            acc = [acc_ref[i, h] for h in heads]
            _, _, _, acc = lax.while_loop(cond, body, (i - 2, jnp.int32(1), later, acc))
            _gated_store(o_ref, z_ref, acc, i)
        return 0

    lax.fori_loop(2, n_blocks, far_tiles, 0)


def _stick_breaking(proj):
    bsz, seq, _ = proj.shape
    n_blocks = seq // TILE
    tri = jnp.asarray(_suffix_matrix(), BF16)
    return pl.pallas_call(
        _sb_kernel,
        grid=(bsz,),
        in_specs=_branch_specs(seq, 4) + [pl.BlockSpec(tri.shape, lambda b: (0, 0))],
        out_specs=pl.BlockSpec((1, seq, WIDTH), lambda b: (b, 0, 0)),
        out_shape=jax.ShapeDtypeStruct((bsz, seq, WIDTH), BF16),
        scratch_shapes=[
            pltpu.VMEM((N_HEADS, n_blocks, HEAD_DIM, TILE), BF16),
            pltpu.VMEM((n_blocks, N_HEADS, LANES, TILE), BF16),
            pltpu.VMEM((n_blocks, N_HEADS, SUBLANES, TILE), F32),
            pltpu.VMEM((n_blocks, N_HEADS, HEAD_DIM, TILE), F32),
            pltpu.SMEM((n_blocks,), jnp.int32),
        ],
        compiler_params=pltpu.CompilerParams(
            dimension_semantics=("arbitrary",), vmem_limit_bytes=VMEM_LIMIT),
        name="stick_breaking",
    )(proj, proj, proj, proj, tri)


def _merge_kernel(x_ref, ua_ref, ub_ref, nw_ref, gb_ref, wg_ref, wua_ref, wub_ref, wo_ref,
                  o_ref, h_ref, y_ref):
    d = x_ref.shape[1]
    h_ref[...] = _rms_rows(x_ref[...], nw_ref[...]).astype(BF16)
    for c in range(0, d, COL_CHUNK):
        cols = slice(c, c + COL_CHUNK)
        cols_b = slice(d + c, d + c + COL_CHUNK)
        ga = _dot(h_ref[...], wg_ref[:, cols]) + gb_ref[:, cols]
        gb = _dot(h_ref[...], wg_ref[:, cols_b]) + gb_ref[:, cols_b]
        ya = _dot(ua_ref[...], wua_ref[:, cols])
        yb = _dot(ub_ref[...], wub_ref[:, cols])
        y = ya / (1.0 + jnp.exp(-ga)) + yb / (1.0 + jnp.exp(-gb))
        y_ref[:, cols] = y.astype(BF16)
    for c in range(0, d, COL_CHUNK):
        cols = slice(c, c + COL_CHUNK)
        o_ref[:, cols] = x_ref[:, cols] + _dot(y_ref[...], wo_ref[:, cols])


def _merge(x2d, ua, ub, norm_w, gate_b, w_in, w_up_a, w_up_b, w_out):
    m, d = x2d.shape
    assert ATTN_COLS % (2 * d) == 0
    row_spec = lambda cols: pl.BlockSpec((PROJ_ROWS, cols), lambda i: (i, 0))
    full = lambda a: pl.BlockSpec(a.shape, lambda i: (0, 0))
    gate_cols = pl.BlockSpec((d, 2 * d), lambda i: (0, ATTN_COLS // (2 * d)))
    return pl.pallas_call(
        _merge_kernel,
        grid=(m // PROJ_ROWS,),
        in_specs=[row_spec(d), row_spec(WIDTH), row_spec(WIDTH), full(norm_w), full(gate_b),
                  gate_cols, full(w_up_a), full(w_up_b), full(w_out)],
        out_specs=row_spec(d),
        out_shape=jax.ShapeDtypeStruct((m, d), F32),
        scratch_shapes=[pltpu.VMEM((PROJ_ROWS, d), BF16), pltpu.VMEM((PROJ_ROWS, d), BF16)],
        compiler_params=pltpu.CompilerParams(
            dimension_semantics=("arbitrary",), vmem_limit_bytes=VMEM_LIMIT),
        name="merge",
    )(x2d, ua, ub, norm_w, gate_b, w_in, w_up_a, w_up_b, w_out)


def kernel(x, norm_w, w_in, merge_gate_b, q_norm_w, k_norm_w, rel_bias, w_up_moba, w_up_sb, w_out):
    bsz, seq, d = x.shape
    assert seq % TILE == 0 and (bsz * seq) % PROJ_ROWS == 0 and d % COL_CHUNK == 0
    assert w_in.shape[2] == ATTN_COLS + 2 * d
    rel_bias = rel_bias.astype(F32)
    bias_tiles = _bias_tiles(rel_bias)
    x2d = x.reshape(bsz * seq, d)
    for l in range(norm_w.shape[0]):
        nw = norm_w[l].reshape(1, d).astype(F32)
        w_l = w_in[l].astype(BF16)
        proj = _project(x2d, nw, w_l).reshape(bsz, seq, ATTN_COLS)
        ua = _moba(proj, q_norm_w[l], k_norm_w[l], rel_bias, bias_tiles).reshape(bsz * seq, WIDTH)
        ub = _stick_breaking(proj).reshape(bsz * seq, WIDTH)
        x2d = _merge(x2d, ua, ub, nw, merge_gate_b[l].reshape(1, 2 * d).astype(F32),
                     w_l, w_up_moba[l].astype(BF16), w_up_sb[l].astype(BF16),
                     w_out[l].astype(BF16))
    return x2d.reshape(bsz, seq, d)
```

```python
import functools
import math

import numpy as np
import jax
import jax.numpy as jnp
from jax import lax
from jax.experimental import pallas as pl
from jax.experimental.pallas import tpu as pltpu

F32 = jnp.float32
BF16 = jnp.bfloat16

HEAD_DIM = 64
N_HEADS = 8
WIDTH = N_HEADS * HEAD_DIM
MOBA_BLOCK = 256
MOBA_TOPK = 3
REL_BUCKETS = 32
REL_MAX_DIST = 128
NORM_EPS = 1e-6
NEG_INF = -1e30
SCALE = HEAD_DIM ** -0.5
LOG2_E = math.log2(math.e)
EXP2_UNDERFLOW = -150.0
EXP2_CLAMP = 126.0

LANES = 128
SUBLANES = 8
TILE = MOBA_BLOCK
PAIRS = WIDTH // LANES
ONES_ROWS = 16
ATTN_COLS = 8 * WIDTH
VMEM_LIMIT = 56 * 1024 * 1024
PROJ_ROWS = 1024
COL_CHUNK = 512


def _dot(a, b):
    return jnp.dot(a, b, preferred_element_type=F32)


def _split_bf16(x):
    hi = x.astype(BF16)
    lo = (x - hi.astype(F32)).astype(BF16)
    return hi, lo


def _rms_rows(x, w):
    ms = jnp.mean(x * x, axis=-1, keepdims=True)
    return x * lax.rsqrt(ms + NORM_EPS) * w


def _proj_kernel(x_ref, nw_ref, w_ref, o_ref, h_ref):
    halves = [slice(r, r + PROJ_ROWS // 2) for r in (0, PROJ_ROWS // 2)]
    h_ref[halves[0], :] = _rms_rows(x_ref[halves[0], :], nw_ref[...]).astype(BF16)
    for k, rows in enumerate(halves):
        for c in range(0, ATTN_COLS, COL_CHUNK):
            o_ref[rows, c:c + COL_CHUNK] = _dot(h_ref[rows, :], w_ref[:, c:c + COL_CHUNK]).astype(BF16)
            if k == 0 and c == 0:
                h_ref[halves[1], :] = _rms_rows(x_ref[halves[1], :], nw_ref[...]).astype(BF16)


def _project(x2d, norm_w, w_in):
    m, d = x2d.shape
    return pl.pallas_call(
        _proj_kernel,
        grid=(m // PROJ_ROWS,),
        in_specs=[
            pl.BlockSpec((PROJ_ROWS, d), lambda i: (i, 0)),
            pl.BlockSpec((1, d), lambda i: (0, 0)),
            pl.BlockSpec((d, ATTN_COLS), lambda i: (0, 0)),
        ],
        out_specs=pl.BlockSpec((PROJ_ROWS, ATTN_COLS), lambda i: (i, 0)),
        out_shape=jax.ShapeDtypeStruct((m, ATTN_COLS), BF16),
        scratch_shapes=[pltpu.VMEM((PROJ_ROWS, d), BF16)],
        compiler_params=pltpu.CompilerParams(
            dimension_semantics=("arbitrary",), vmem_limit_bytes=VMEM_LIMIT),
        name="proj",
    )(x2d, norm_w, w_in)


def _t5_bucket_np(dist):
    n = np.maximum(dist, 0)
    max_exact = REL_BUCKETS // 2
    nf = np.maximum(n, 1).astype(np.float64)
    large = max_exact + (np.log(nf / max_exact) / math.log(REL_MAX_DIST / max_exact)
                         * (REL_BUCKETS - max_exact)).astype(np.int32)
    large = np.minimum(large, REL_BUCKETS - 1)
    return np.where(n < max_exact, n, large).astype(np.int32)


def _bucket_tiles():
    key = np.arange(TILE)[:, None]
    qry = np.arange(TILE)[None, :]
    own = np.where(qry >= key, _t5_bucket_np(qry - key), REL_BUCKETS)
    prev = _t5_bucket_np(TILE + qry - key)
    assert (_t5_bucket_np(2 * TILE + qry - key) == REL_BUCKETS - 1).all()
    return np.stack([own, prev]).astype(np.int32)


def _bias_kernel(rel_ref, bkt_ref, o_ref):
    h = pl.program_id(0)
    for t in range(2):
        bkt = bkt_ref[t]
        acc = jnp.full((TILE, TILE), NEG_INF, F32)
        for b in range(REL_BUCKETS):
            acc = jnp.where(bkt == b, rel_ref[h, b] * LOG2_E, acc)
        o_ref[0, t] = acc


def _bias_tiles(rel_bias):
    return pl.pallas_call(
        _bias_kernel,
        grid=(N_HEADS,),
        in_specs=[
            pl.BlockSpec(memory_space=pltpu.SMEM),
            pl.BlockSpec((2, TILE, TILE), lambda h: (0, 0, 0)),
        ],
        out_specs=pl.BlockSpec((1, 2, TILE, TILE), lambda h: (h, 0, 0, 0)),
        out_shape=jax.ShapeDtypeStruct((N_HEADS, 2, TILE, TILE), F32),
        compiler_params=pltpu.CompilerParams(dimension_semantics=("arbitrary",)),
        name="bias_tiles",
    )(rel_bias, jnp.asarray(_bucket_tiles()))


def _gated_store(o_ref, z_ref, out_t, tile):
    rows = pl.ds(pl.multiple_of(tile * TILE, TILE), TILE)
    z = z_ref[0, rows, :].astype(F32)
    gate = z / (1.0 + jnp.exp(-z))
    for p in range(PAIRS):
        lanes = slice(p * LANES, (p + 1) * LANES)
        pair_t = jnp.concatenate([out_t[2 * p], out_t[2 * p + 1]], axis=0)
        o_ref[0, rows, lanes] = (pair_t.T * gate[:, lanes]).astype(BF16)


def _interleave(*stage_lists):
    todo = [list(s) for s in stage_lists]
    done = [0] * len(todo)
    while any(done[k] < len(todo[k]) for k in range(len(todo))):
        k = min((k for k in range(len(todo)) if done[k] < len(todo[k])),
                key=lambda k: (done[k] + 0.5) / len(todo[k]))
        todo[k][done[k]]()
        done[k] += 1


def _moba_kernel(rel_ref, q_ref, k_ref, v_ref, z_ref, qw_ref, kw_ref, bias_ref, o_ref,
                 kaug_ref, vt_ref, kmean_ref, qaug_ref):
    n_blocks = k_ref.shape[1] // TILE
    pairs = range(PAIRS)

    lane = lax.broadcasted_iota(jnp.int32, (TILE, LANES), 1)
    head0 = lane < HEAD_DIM
    ones = jnp.ones((ONES_ROWS, TILE), BF16)
    for blk in range(n_blocks):
        rows = slice(blk * TILE, (blk + 1) * TILE)
        hot0 = ((lane == HEAD_DIM + blk) | (lane == HEAD_DIM + SUBLANES + blk)).astype(F32)
        hot1 = ((lane == blk) | (lane == SUBLANES + blk)).astype(F32)
        for p in pairs:
            lanes = slice(p * LANES, (p + 1) * LANES)
            kb = k_ref[0, rows, lanes].astype(F32)
            sq = kb * kb
            ss_all = jnp.sum(sq, axis=-1, keepdims=True)
            ss0 = jnp.sum(jnp.where(head0, sq, 0.0), axis=-1, keepdims=True)
            r0 = lax.rsqrt(ss0 * (1.0 / HEAD_DIM) + NORM_EPS)
            r1 = lax.rsqrt((ss_all - ss0) * (1.0 / HEAD_DIM) + NORM_EPS)
            khat = kb * jnp.where(head0, r0, r1) * kw_ref[...]
            kmean_ref[blk:blk + 1, lanes] = jnp.mean(khat, axis=0, keepdims=True)
            kaug_ref[2 * p, blk] = jnp.where(head0, khat, hot0).astype(BF16)
            kaug_ref[2 * p + 1, blk] = jnp.where(head0, hot1, khat).astype(BF16)
            vt = v_ref[0, rows, lanes].astype(F32).T
            vt_ref[2 * p, blk, 0:HEAD_DIM, :] = vt[0:HEAD_DIM].astype(BF16)
            vt_ref[2 * p, blk, HEAD_DIM:, :] = ones
            vt_ref[2 * p + 1, blk, 0:HEAD_DIM, :] = vt[HEAD_DIM:].astype(BF16)
            vt_ref[2 * p + 1, blk, HEAD_DIM:, :] = ones

    row = lax.broadcasted_iota(jnp.int32, (LANES, TILE), 0)
    top = row < HEAD_DIM
    lane8 = lax.broadcasted_iota(jnp.int32, (n_blocks, LANES), 1)
    units = [(i, p) for i in range(n_blocks) for p in pairs]
    qts = [q_ref[0, i * TILE:(i + 1) * TILE, p * LANES:(p + 1) * LANES].astype(F32).T
           for i, p in units]
    qhats = []
    for qt in qts:
        sq = qt * qt
        r0 = lax.rsqrt(jnp.sum(sq[0:HEAD_DIM], axis=0, keepdims=True) * (1.0 / HEAD_DIM) + NORM_EPS)
        r1 = lax.rsqrt(jnp.sum(sq[HEAD_DIM:], axis=0, keepdims=True) * (1.0 / HEAD_DIM) + NORM_EPS)
        qhats.append(qt * jnp.where(top, r0, r1) * qw_ref[...])
    gates = []
    for (i, p), qhat in zip(units, qhats):
        km = kmean_ref[:, p * LANES:(p + 1) * LANES]
        km2 = jnp.concatenate([jnp.where(lane8 < HEAD_DIM, km, 0.0),
                               jnp.where(lane8 < HEAD_DIM, 0.0, km)], axis=0)
        kh, kl = _split_bf16(km2)
        qh, ql = _split_bf16(qhat)
        gates.append(_dot(kh, qh) + _dot(kh, ql) + _dot(kl, qh))

    def block_terms(g, far_bias, i):
        n = lax.broadcasted_iota(jnp.int32, g.shape, 0)
        cnt = jnp.zeros(g.shape, jnp.int32)
        for m in range(i):
            gm = g[m:m + 1, :]
            beats = (gm > g) | ((gm == g) & (m < n))
            cnt = cnt + jnp.where(beats, 1, 0)
        keep = ((cnt < MOBA_TOPK) & (n < i)) | (n == i)
        hi, lo = _split_bf16(jnp.where(keep, jnp.where(n < i - 1, far_bias, 0.0), NEG_INF))
        return [hi.astype(F32), lo.astype(F32)]

    pad = jnp.zeros((HEAD_DIM - 2 * n_blocks, TILE), F32)
    for (i, p), qhat, gate in zip(units, qhats, gates):
        qs = qhat * (SCALE * LOG2_E)
        far0 = rel_ref[2 * p, REL_BUCKETS - 1] * LOG2_E
        far1 = rel_ref[2 * p + 1, REL_BUCKETS - 1] * LOG2_E
        qaug_ref[i, 2 * p] = jnp.concatenate(
            [qs[0:HEAD_DIM]] + block_terms(gate[0:n_blocks], far0, i) + [pad], axis=0).astype(BF16)
        qaug_ref[i, 2 * p + 1] = jnp.concatenate(
            block_terms(gate[n_blocks:], far1, i) + [pad, qs[HEAD_DIM:]], axis=0).astype(BF16)

    def attend_stages(i):
        s, p, out = {}, {}, {}

        def stage(t):
            if t < N_HEADS:
                qa = qaug_ref[i, t]
                s[t] = []
                for j in range(i + 1):
                    sj = _dot(kaug_ref[t, j], qa)
                    if i - j < 2:
                        sj = sj + bias_ref[t, i - j]
                    s[t].append(sj)
            if t >= 2:
                h = t - 2
                acc = None
                for j, pj in enumerate(p.pop(h)):
                    pv = _dot(vt_ref[h, j], pj)
                    acc = pv if acc is None else acc + pv
                out[h] = acc[0:HEAD_DIM] * (1.0 / acc[HEAD_DIM:HEAD_DIM + 1])
            if 1 <= t <= N_HEADS:
                h = t - 1
                m = functools.reduce(jnp.maximum, [jnp.max(sj, axis=0, keepdims=True) for sj in s[h]])
                p[h] = [jnp.exp2(sj - m).astype(BF16) for sj in s.pop(h)]
            if t == N_HEADS + 1:
                _gated_store(o_ref, z_ref, out, i)

        return [functools.partial(stage, t) for t in range(N_HEADS + 2)]

    _interleave(*[attend_stages(i) for i in range(n_blocks)])


def _branch_specs(seq, first):
    return [pl.BlockSpec((1, seq, WIDTH), functools.partial(lambda c, b: (b, 0, c), first + c))
            for c in range(4)]


def _moba(proj, q_norm_w, k_norm_w, rel_bias, bias_tiles):
    bsz, seq, _ = proj.shape
    n_blocks = seq // TILE
    assert 2 * n_blocks <= HEAD_DIM
    qw = jnp.tile(q_norm_w.astype(F32), 2).reshape(LANES, 1)
    kw = jnp.tile(k_norm_w.astype(F32), 2).reshape(1, LANES)
    return pl.pallas_call(
        _moba_kernel,
        grid=(bsz,),
        in_specs=[pl.BlockSpec(memory_space=pltpu.SMEM)] + _branch_specs(seq, 0) + [
            pl.BlockSpec((LANES, 1), lambda b: (0, 0)),
            pl.BlockSpec((1, LANES), lambda b: (0, 0)),
            pl.BlockSpec(bias_tiles.shape, lambda b: (0, 0, 0, 0)),
        ],
        out_specs=pl.BlockSpec((1, seq, WIDTH), lambda b: (b, 0, 0)),
        out_shape=jax.ShapeDtypeStruct((bsz, seq, WIDTH), BF16),
        scratch_shapes=[
            pltpu.VMEM((N_HEADS, n_blocks, TILE, LANES), BF16),
            pltpu.VMEM((N_HEADS, n_blocks, HEAD_DIM + ONES_ROWS, TILE), BF16),
            pltpu.VMEM((n_blocks, WIDTH), F32),
            pltpu.VMEM((n_blocks, N_HEADS, LANES, TILE), BF16),
        ],
        compiler_params=pltpu.CompilerParams(
            dimension_semantics=("arbitrary",), vmem_limit_bytes=VMEM_LIMIT),
        name="moba",
    )(rel_bias, proj, proj, proj, proj, qw, kw, bias_tiles)


def _suffix_matrix():
    s = np.arange(TILE)[:, None]
    j = np.arange(TILE)[None, :]
    return np.concatenate([np.where(j > s, -1.0, 0.0), -np.ones((ONES_ROWS, TILE))]).astype(np.float32)


def _sb_kernel(q_ref, k_ref, v_ref, z_ref, tri_ref, o_ref,
               vt_ref, qs_ref, later_ref, acc_ref, alive_ref):
    n_blocks = k_ref.shape[1] // TILE
    heads = range(N_HEADS)

    for blk in range(n_blocks):
        for p in range(PAIRS):
            vt = v_ref[0, blk * TILE:(blk + 1) * TILE, p * LANES:(p + 1) * LANES].astype(F32).T
            vt_ref[2 * p, blk] = vt[0:HEAD_DIM].astype(BF16)
            vt_ref[2 * p + 1, blk] = vt[HEAD_DIM:].astype(BF16)

    row = lax.broadcasted_iota(jnp.int32, (LANES, TILE), 0)
    for i in range(n_blocks):
        for p in range(PAIRS):
            qt = q_ref[0, i * TILE:(i + 1) * TILE, p * LANES:(p + 1) * LANES].astype(F32).T
            qt = qt * (SCALE * LOG2_E)
            qs_ref[i, 2 * p] = jnp.where(row < HEAD_DIM, qt, 0.0).astype(BF16)
            qs_ref[i, 2 * p + 1] = jnp.where(row < HEAD_DIM, 0.0, qt).astype(BF16)
    key = lax.broadcasted_iota(jnp.int32, (TILE, TILE), 0)
    qry = lax.broadcasted_iota(jnp.int32, (TILE, TILE), 1)
    past = key < qry

    def scores(j, h, i):
        lanes = slice((h // 2) * LANES, (h // 2 + 1) * LANES)
        kb = k_ref[0, pl.ds(pl.multiple_of(j * TILE, TILE), TILE), lanes]
        return _dot(kb, qs_ref[i, h])

    def softplus2(z2):
        return jnp.maximum(z2, jnp.log2(1.0 + jnp.exp2(jnp.minimum(z2, EXP2_CLAMP))))

    def suffix(nl):
        return _dot(tri_ref[...], nl.astype(BF16))

    chains = [(i, h, j) for i in range(n_blocks) for h in heads for j in range(i, max(i - 2, -1), -1)]
    n = len(chains)
    zs, nls, lw, sufs, ws, pvs = {}, {}, {}, {}, {}, {}
    for t in range(n + 4):
        if t < n:
            i, h, j = chains[t]
            zs[t] = scores(j, h, i)
        if 0 <= t - 2 < n:
            sufs[t - 2] = suffix(nls.pop(t - 2))
        if 0 <= t - 4 < n:
            c = t - 4
            i, h, j = chains[c]
            pvs[c] = _dot(vt_ref[h, j], ws.pop(c))
        if 0 <= t - 1 < n:
            c = t - 1
            i, h, j = chains[c]
            nl = softplus2(zs[c])
            if j == i:
                nl = jnp.where(past, nl, 0.0)
            lw[c] = zs.pop(c) - nl
            nls[c] = nl
        if 0 <= t - 3 < n:
            c = t - 3
            i, h, j = chains[c]
            if j == i:
                w = jnp.where(past, jnp.exp2(lw.pop(c) + sufs[c][0:TILE]), 0.0)
            else:
                w = jnp.exp2(lw.pop(c) + sufs[c][0:TILE] + sufs[c - 1][TILE:TILE + 1])
            ws[c] = w.astype(BF16)

    by_tile = {}
    for c, (i, h, j) in enumerate(chains):
        by_tile.setdefault((i, h), []).append(c)
    for i in range(n_blocks):
        later = [sum(sufs[c][TILE:TILE + SUBLANES] for c in by_tile[i, h]) for h in heads]
        acc = [sum(pvs[c] for c in by_tile[i, h]) for h in heads]
        _gated_store(o_ref, z_ref, acc, i)
        if i >= 2:
            for h in heads:
                later_ref[i, h] = later[h]
                acc_ref[i, h] = acc[h]
            top_later = functools.reduce(jnp.maximum, later)
            alive_ref[i] = (jnp.max(top_later) > EXP2_UNDERFLOW).astype(jnp.int32)

    def far_tiles(i, _):
        @pl.when(alive_ref[i] > 0)
        def _():
            def cond(carry):
                return (carry[0] >= 0) & (carry[1] > 0)

            def body(carry):
                j, _, later, acc = carry
                new_later, new_acc = [], []
                for h in heads:
                    z = scores(j, h, i)
                    nl = softplus2(z)
                    suf = suffix(nl)
                    w = jnp.exp2((z - nl) + suf[0:TILE] + later[h])
                    new_later.append(later[h] + suf[TILE:TILE + 1])
                    new_acc.append(acc[h] + _dot(vt_ref[h, j], w.astype(BF16)))
                top_later = functools.reduce(jnp.maximum, new_later)
                alive = (jnp.max(top_later) > EXP2_UNDERFLOW).astype(jnp.int32)
                return j - 1, alive, new_later, new_acc

            later = [later_ref[i, h][0:1] for h in heads]
            acc = [acc_ref[i, h] for h in heads]
            _, _, _, acc = lax.while_loop(cond, body, (i - 2, jnp.int32(1), later, acc))
            _gated_store(o_ref, z_ref, acc, i)
        return 0

    lax.fori_loop(2, n_blocks, far_tiles, 0)


def _stick_breaking(proj):
    bsz, seq, _ = proj.shape
    n_blocks = seq // TILE
    tri = jnp.asarray(_suffix_matrix(), BF16)
    return pl.pallas_call(
        _sb_kernel,
        grid=(bsz,),
        in_specs=_branch_specs(seq, 4) + [pl.BlockSpec(tri.shape, lambda b: (0, 0))],
        out_specs=pl.BlockSpec((1, seq, WIDTH), lambda b: (b, 0, 0)),
        out_shape=jax.ShapeDtypeStruct((bsz, seq, WIDTH), BF16),
        scratch_shapes=[
            pltpu.VMEM((N_HEADS, n_blocks, HEAD_DIM, TILE), BF16),
            pltpu.VMEM((n_blocks, N_HEADS, LANES, TILE), BF16),
            pltpu.VMEM((n_blocks, N_HEADS, SUBLANES, TILE), F32),
            pltpu.VMEM((n_blocks, N_HEADS, HEAD_DIM, TILE), F32),
            pltpu.SMEM((n_blocks,), jnp.int32),
        ],
        compiler_params=pltpu.CompilerParams(
            dimension_semantics=("arbitrary",), vmem_limit_bytes=VMEM_LIMIT),
        name="stick_breaking",
    )(proj, proj, proj, proj, tri)


def _merge_kernel(x_ref, ua_ref, ub_ref, nw_ref, gb_ref, wg_ref, wua_ref, wub_ref, wo_ref,
                  o_ref, h_ref, y_ref):
    d = x_ref.shape[1]
    halves = [slice(r, r + PROJ_ROWS // 2) for r in (0, PROJ_ROWS // 2)]
    h_ref[halves[0], :] = _rms_rows(x_ref[halves[0], :], nw_ref[...]).astype(BF16)
    for k, rows in enumerate(halves):
        for c in range(0, d, COL_CHUNK):
            cols = slice(c, c + COL_CHUNK)
            cols_b = slice(d + c, d + c + COL_CHUNK)
            ga = _dot(h_ref[rows, :], wg_ref[:, cols]) + gb_ref[:, cols]
            gb = _dot(h_ref[rows, :], wg_ref[:, cols_b]) + gb_ref[:, cols_b]
            ya = _dot(ua_ref[rows, :], wua_ref[:, cols])
            yb = _dot(ub_ref[rows, :], wub_ref[:, cols])
            y = ya / (1.0 + jnp.exp(-ga)) + yb / (1.0 + jnp.exp(-gb))
            y_ref[rows, cols] = y.astype(BF16)
            if k == 0 and c == 0:
                h_ref[halves[1], :] = _rms_rows(x_ref[halves[1], :], nw_ref[...]).astype(BF16)
    for rows in halves:
        for c in range(0, d, COL_CHUNK):
            cols = slice(c, c + COL_CHUNK)
            o_ref[rows, cols] = x_ref[rows, cols] + _dot(y_ref[rows, :], wo_ref[:, cols])


def _merge(x2d, ua, ub, norm_w, gate_b, w_in, w_up_a, w_up_b, w_out):
    m, d = x2d.shape
    assert ATTN_COLS % (2 * d) == 0
    row_spec = lambda cols: pl.BlockSpec((PROJ_ROWS, cols), lambda i: (i, 0))
    full = lambda a: pl.BlockSpec(a.shape, lambda i: (0, 0))
    gate_cols = pl.BlockSpec((d, 2 * d), lambda i: (0, ATTN_COLS // (2 * d)))
    return pl.pallas_call(
        _merge_kernel,
        grid=(m // PROJ_ROWS,),
        in_specs=[row_spec(d), row_spec(WIDTH), row_spec(WIDTH), full(norm_w), full(gate_b),
                  gate_cols, full(w_up_a), full(w_up_b), full(w_out)],
        out_specs=row_spec(d),
        out_shape=jax.ShapeDtypeStruct((m, d), F32),
        scratch_shapes=[pltpu.VMEM((PROJ_ROWS, d), BF16), pltpu.VMEM((PROJ_ROWS, d), BF16)],
        compiler_params=pltpu.CompilerParams(
            dimension_semantics=("arbitrary",), vmem_limit_bytes=VMEM_LIMIT),
        name="merge",
    )(x2d, ua, ub, norm_w, gate_b, w_in, w_up_a, w_up_b, w_out)


def kernel(x, norm_w, w_in, merge_gate_b, q_norm_w, k_norm_w, rel_bias, w_up_moba, w_up_sb, w_out):
    bsz, seq, d = x.shape
    assert seq % TILE == 0 and (bsz * seq) % PROJ_ROWS == 0 and d % COL_CHUNK == 0
    assert w_in.shape[2] == ATTN_COLS + 2 * d
    rel_bias = rel_bias.astype(F32)
    bias_tiles = _bias_tiles(rel_bias)
    x2d = x.reshape(bsz * seq, d)
    for l in range(norm_w.shape[0]):
        nw = norm_w[l].reshape(1, d).astype(F32)
        w_l = w_in[l].astype(BF16)
        proj = _project(x2d, nw, w_l).reshape(bsz, seq, ATTN_COLS)
        ua = _moba(proj, q_norm_w[l], k_norm_w[l], rel_bias, bias_tiles).reshape(bsz * seq, WIDTH)
        ub = _stick_breaking(proj).reshape(bsz * seq, WIDTH)
        x2d = _merge(x2d, ua, ub, nw, merge_gate_b[l].reshape(1, 2 * d).astype(F32),
                     w_l, w_up_moba[l].astype(BF16), w_up_sb[l].astype(BF16),
                     w_out[l].astype(BF16))
    return x2d.reshape(bsz, seq, d)
```

```python
import functools
import math

import numpy as np
import jax
import jax.numpy as jnp
from jax import lax
from jax.experimental import pallas as pl
from jax.experimental.pallas import tpu as pltpu

F32 = jnp.float32
BF16 = jnp.bfloat16

HEAD_DIM = 64
N_HEADS = 8
WIDTH = N_HEADS * HEAD_DIM
MOBA_BLOCK = 256
MOBA_TOPK = 3
REL_BUCKETS = 32
REL_MAX_DIST = 128
NORM_EPS = 1e-6
NEG_INF = -1e30
SCALE = HEAD_DIM ** -0.5
LOG2_E = math.log2(math.e)
EXP2_UNDERFLOW = -150.0
EXP2_CLAMP = 126.0

LANES = 128
SUBLANES = 8
TILE = MOBA_BLOCK
PAIRS = WIDTH // LANES
ONES_ROWS = 16
ATTN_COLS = 8 * WIDTH
VMEM_LIMIT = 56 * 1024 * 1024
PROJ_ROWS = 1024
COL_CHUNK = 512


def _dot(a, b):
    return jnp.dot(a, b, preferred_element_type=F32)


def _split_bf16(x):
    hi = x.astype(BF16)
    lo = (x - hi.astype(F32)).astype(BF16)
    return hi, lo


def _rms_rows(x, w):
    ms = jnp.mean(x * x, axis=-1, keepdims=True)
    return x * lax.rsqrt(ms + NORM_EPS) * w


def _proj_kernel(x_ref, nw_ref, w_ref, o_ref, h_ref):
    halves = [slice(r, r + PROJ_ROWS // 2) for r in (0, PROJ_ROWS // 2)]
    h_ref[halves[0], :] = _rms_rows(x_ref[halves[0], :], nw_ref[...]).astype(BF16)
    for k, rows in enumerate(halves):
        for c in range(0, ATTN_COLS, COL_CHUNK):
            o_ref[rows, c:c + COL_CHUNK] = _dot(h_ref[rows, :], w_ref[:, c:c + COL_CHUNK]).astype(BF16)
            if k == 0 and c == 0:
                h_ref[halves[1], :] = _rms_rows(x_ref[halves[1], :], nw_ref[...]).astype(BF16)


def _project(x2d, norm_w, w_in):
    m, d = x2d.shape
    return pl.pallas_call(
        _proj_kernel,
        grid=(m // PROJ_ROWS,),
        in_specs=[
            pl.BlockSpec((PROJ_ROWS, d), lambda i: (i, 0)),
            pl.BlockSpec((1, d), lambda i: (0, 0)),
            pl.BlockSpec((d, ATTN_COLS), lambda i: (0, 0)),
        ],
        out_specs=pl.BlockSpec((PROJ_ROWS, ATTN_COLS), lambda i: (i, 0)),
        out_shape=jax.ShapeDtypeStruct((m, ATTN_COLS), BF16),
        scratch_shapes=[pltpu.VMEM((PROJ_ROWS, d), BF16)],
        compiler_params=pltpu.CompilerParams(
            dimension_semantics=("arbitrary",), vmem_limit_bytes=VMEM_LIMIT),
        name="proj",
    )(x2d, norm_w, w_in)


def _t5_bucket_np(dist):
    n = np.maximum(dist, 0)
    max_exact = REL_BUCKETS // 2
    nf = np.maximum(n, 1).astype(np.float64)
    large = max_exact + (np.log(nf / max_exact) / math.log(REL_MAX_DIST / max_exact)
                         * (REL_BUCKETS - max_exact)).astype(np.int32)
    large = np.minimum(large, REL_BUCKETS - 1)
    return np.where(n < max_exact, n, large).astype(np.int32)


def _bucket_tiles():
    key = np.arange(TILE)[:, None]
    qry = np.arange(TILE)[None, :]
    own = np.where(qry >= key, _t5_bucket_np(qry - key), REL_BUCKETS)
    prev = _t5_bucket_np(TILE + qry - key)
    assert (_t5_bucket_np(2 * TILE + qry - key) == REL_BUCKETS - 1).all()
    return np.stack([own, prev]).astype(np.int32)


def _bias_kernel(rel_ref, bkt_ref, o_ref):
    for t in range(2):
        bkt = bkt_ref[t]
        hits = [bkt == b for b in range(REL_BUCKETS)]
        for h in range(N_HEADS):
            acc = jnp.full((TILE, TILE), NEG_INF, F32)
            for b in range(REL_BUCKETS):
                acc = jnp.where(hits[b], rel_ref[h, b] * LOG2_E, acc)
            o_ref[h, t] = acc


def _bias_tiles(rel_bias):
    shape = (N_HEADS, 2, TILE, TILE)
    return pl.pallas_call(
        _bias_kernel,
        grid=(1,),
        in_specs=[
            pl.BlockSpec(memory_space=pltpu.SMEM),
            pl.BlockSpec((2, TILE, TILE), lambda g: (0, 0, 0)),
        ],
        out_specs=pl.BlockSpec(shape, lambda g: (0, 0, 0, 0)),
        out_shape=jax.ShapeDtypeStruct(shape, F32),
        compiler_params=pltpu.CompilerParams(dimension_semantics=("arbitrary",)),
        name="bias_tiles",
    )(rel_bias, jnp.asarray(_bucket_tiles()))


def _gated_store(o_ref, z_ref, out_t, tile):
    rows = pl.ds(pl.multiple_of(tile * TILE, TILE), TILE)
    z = z_ref[0, rows, :].astype(F32)
    gate = z / (1.0 + jnp.exp(-z))
    for p in range(PAIRS):
        lanes = slice(p * LANES, (p + 1) * LANES)
        pair_t = jnp.concatenate([out_t[2 * p], out_t[2 * p + 1]], axis=0)
        o_ref[0, rows, lanes] = (pair_t.T * gate[:, lanes]).astype(BF16)


def _interleave(*stage_lists):
    todo = [list(s) for s in stage_lists]
    done = [0] * len(todo)
    while any(done[k] < len(todo[k]) for k in range(len(todo))):
        k = min((k for k in range(len(todo)) if done[k] < len(todo[k])),
                key=lambda k: (done[k] + 0.5) / len(todo[k]))
        todo[k][done[k]]()
        done[k] += 1


def _moba_kernel(rel_ref, q_ref, k_ref, v_ref, z_ref, qw_ref, kw_ref, bias_ref, o_ref,
                 kaug_ref, vt_ref, kmean_ref, qaug_ref):
    n_blocks = k_ref.shape[1] // TILE
    pairs = range(PAIRS)

    lane = lax.broadcasted_iota(jnp.int32, (TILE, LANES), 1)
    head0 = lane < HEAD_DIM
    ones = jnp.ones((ONES_ROWS, TILE), BF16)
    for blk in range(n_blocks):
        rows = slice(blk * TILE, (blk + 1) * TILE)
        hot0 = ((lane == HEAD_DIM + blk) | (lane == HEAD_DIM + SUBLANES + blk)).astype(F32)
        hot1 = ((lane == blk) | (lane == SUBLANES + blk)).astype(F32)
        for p in pairs:
            lanes = slice(p * LANES, (p + 1) * LANES)
            kb = k_ref[0, rows, lanes].astype(F32)
            sq = kb * kb
            ss_all = jnp.sum(sq, axis=-1, keepdims=True)
            ss0 = jnp.sum(jnp.where(head0, sq, 0.0), axis=-1, keepdims=True)
            r0 = lax.rsqrt(ss0 * (1.0 / HEAD_DIM) + NORM_EPS)
            r1 = lax.rsqrt((ss_all - ss0) * (1.0 / HEAD_DIM) + NORM_EPS)
            khat = kb * jnp.where(head0, r0, r1) * kw_ref[...]
            kmean_ref[blk:blk + 1, lanes] = jnp.mean(khat, axis=0, keepdims=True)
            kaug_ref[2 * p, blk] = jnp.where(head0, khat, hot0).astype(BF16)
            kaug_ref[2 * p + 1, blk] = jnp.where(head0, hot1, khat).astype(BF16)
            vt = v_ref[0, rows, lanes].astype(F32).T
            vt_ref[2 * p, blk, 0:HEAD_DIM, :] = vt[0:HEAD_DIM].astype(BF16)
            vt_ref[2 * p, blk, HEAD_DIM:, :] = ones
            vt_ref[2 * p + 1, blk, 0:HEAD_DIM, :] = vt[HEAD_DIM:].astype(BF16)
            vt_ref[2 * p + 1, blk, HEAD_DIM:, :] = ones

    row = lax.broadcasted_iota(jnp.int32, (LANES, TILE), 0)
    top = row < HEAD_DIM
    lane8 = lax.broadcasted_iota(jnp.int32, (n_blocks, LANES), 1)
    units = [(i, p) for i in range(n_blocks) for p in pairs]
    qts = [q_ref[0, i * TILE:(i + 1) * TILE, p * LANES:(p + 1) * LANES].astype(F32).T
           for i, p in units]
    qhats = []
    for qt in qts:
        sq = qt * qt
        r0 = lax.rsqrt(jnp.sum(sq[0:HEAD_DIM], axis=0, keepdims=True) * (1.0 / HEAD_DIM) + NORM_EPS)
        r1 = lax.rsqrt(jnp.sum(sq[HEAD_DIM:], axis=0, keepdims=True) * (1.0 / HEAD_DIM) + NORM_EPS)
        qhats.append(qt * jnp.where(top, r0, r1) * qw_ref[...])
    gates = []
    for (i, p), qhat in zip(units, qhats):
        km = kmean_ref[:, p * LANES:(p + 1) * LANES]
        km2 = jnp.concatenate([jnp.where(lane8 < HEAD_DIM, km, 0.0),
                               jnp.where(lane8 < HEAD_DIM, 0.0, km)], axis=0)
        kh, kl = _split_bf16(km2)
        qh, ql = _split_bf16(qhat)
        gates.append(_dot(kh, qh) + _dot(kh, ql) + _dot(kl, qh))

    def block_terms(g, far_bias, i):
        n = lax.broadcasted_iota(jnp.int32, g.shape, 0)
        cnt = jnp.zeros(g.shape, jnp.int32)
        for m in range(i):
            gm = g[m:m + 1, :]
            beats = (gm > g) | ((gm == g) & (m < n))
            cnt = cnt + jnp.where(beats, 1, 0)
        keep = ((cnt < MOBA_TOPK) & (n < i)) | (n == i)
        hi, lo = _split_bf16(jnp.where(keep, jnp.where(n < i - 1, far_bias, 0.0), NEG_INF))
        return [hi.astype(F32), lo.astype(F32)]

    pad = jnp.zeros((HEAD_DIM - 2 * n_blocks, TILE), F32)
    for (i, p), qhat, gate in zip(units, qhats, gates):
        qs = qhat * (SCALE * LOG2_E)
        far0 = rel_ref[2 * p, REL_BUCKETS - 1] * LOG2_E
        far1 = rel_ref[2 * p + 1, REL_BUCKETS - 1] * LOG2_E
        qaug_ref[i, 2 * p] = jnp.concatenate(
            [qs[0:HEAD_DIM]] + block_terms(gate[0:n_blocks], far0, i) + [pad], axis=0).astype(BF16)
        qaug_ref[i, 2 * p + 1] = jnp.concatenate(
            block_terms(gate[n_blocks:], far1, i) + [pad, qs[HEAD_DIM:]], axis=0).astype(BF16)

    def attend_stages(i):
        s, p, out = {}, {}, {}

        def stage(t):
            if t < N_HEADS:
                qa = qaug_ref[i, t]
                s[t] = []
                for j in range(i + 1):
                    sj = _dot(kaug_ref[t, j], qa)
                    if i - j < 2:
                        sj = sj + bias_ref[t, i - j]
                    s[t].append(sj)
            if t >= 2:
                h = t - 2
                acc = None
                for j, pj in enumerate(p.pop(h)):
                    pv = _dot(vt_ref[h, j], pj)
                    acc = pv if acc is None else acc + pv
                out[h] = acc[0:HEAD_DIM] * (1.0 / acc[HEAD_DIM:HEAD_DIM + 1])
            if 1 <= t <= N_HEADS:
                h = t - 1
                m = functools.reduce(jnp.maximum, [jnp.max(sj, axis=0, keepdims=True) for sj in s[h]])
                p[h] = [jnp.exp2(sj - m).astype(BF16) for sj in s.pop(h)]
            if t == N_HEADS + 1:
                _gated_store(o_ref, z_ref, out, i)

        return [functools.partial(stage, t) for t in range(N_HEADS + 2)]

    _interleave(*[attend_stages(i) for i in range(n_blocks)])


def _branch_specs(seq, first):
    return [pl.BlockSpec((1, seq, WIDTH), functools.partial(lambda c, b: (b, 0, c), first + c))
            for c in range(4)]


def _moba(proj, q_norm_w, k_norm_w, rel_bias, bias_tiles):
    bsz, seq, _ = proj.shape
    n_blocks = seq // TILE
    assert 2 * n_blocks <= HEAD_DIM
    qw = jnp.tile(q_norm_w.astype(F32), 2).reshape(LANES, 1)
    kw = jnp.tile(k_norm_w.astype(F32), 2).reshape(1, LANES)
    return pl.pallas_call(
        _moba_kernel,
        grid=(bsz,),
        in_specs=[pl.BlockSpec(memory_space=pltpu.SMEM)] + _branch_specs(seq, 0) + [
            pl.BlockSpec((LANES, 1), lambda b: (0, 0)),
            pl.BlockSpec((1, LANES), lambda b: (0, 0)),
            pl.BlockSpec(bias_tiles.shape, lambda b: (0, 0, 0, 0)),
        ],
        out_specs=pl.BlockSpec((1, seq, WIDTH), lambda b: (b, 0, 0)),
        out_shape=jax.ShapeDtypeStruct((bsz, seq, WIDTH), BF16),
        scratch_shapes=[
            pltpu.VMEM((N_HEADS, n_blocks, TILE, LANES), BF16),
            pltpu.VMEM((N_HEADS, n_blocks, HEAD_DIM + ONES_ROWS, TILE), BF16),
            pltpu.VMEM((n_blocks, WIDTH), F32),
            pltpu.VMEM((n_blocks, N_HEADS, LANES, TILE), BF16),
        ],
        compiler_params=pltpu.CompilerParams(
            dimension_semantics=("arbitrary",), vmem_limit_bytes=VMEM_LIMIT),
        name="moba",
    )(rel_bias, proj, proj, proj, proj, qw, kw, bias_tiles)


def _suffix_matrix():
    s = np.arange(TILE)[:, None]
    j = np.arange(TILE)[None, :]
    return np.concatenate([np.where(j > s, -1.0, 0.0), -np.ones((ONES_ROWS, TILE))]).astype(np.float32)


def _sb_kernel(q_ref, k_ref, v_ref, z_ref, tri_ref, o_ref,
               vt_ref, qs_ref, later_ref, acc_ref, alive_ref):
    n_blocks = k_ref.shape[1] // TILE
    heads = range(N_HEADS)

    for blk in range(n_blocks):
        for p in range(PAIRS):
            vt = v_ref[0, blk * TILE:(blk + 1) * TILE, p * LANES:(p + 1) * LANES].astype(F32).T
            vt_ref[2 * p, blk] = vt[0:HEAD_DIM].astype(BF16)
            vt_ref[2 * p + 1, blk] = vt[HEAD_DIM:].astype(BF16)

    row = lax.broadcasted_iota(jnp.int32, (LANES, TILE), 0)
    for i in range(n_blocks):
        for p in range(PAIRS):
            qt = q_ref[0, i * TILE:(i + 1) * TILE, p * LANES:(p + 1) * LANES].astype(F32).T
            qt = qt * (SCALE * LOG2_E)
            qs_ref[i, 2 * p] = jnp.where(row < HEAD_DIM, qt, 0.0).astype(BF16)
            qs_ref[i, 2 * p + 1] = jnp.where(row < HEAD_DIM, 0.0, qt).astype(BF16)
    key = lax.broadcasted_iota(jnp.int32, (TILE, TILE), 0)
    qry = lax.broadcasted_iota(jnp.int32, (TILE, TILE), 1)
    past = key < qry

    def scores(j, h, i):
        lanes = slice((h // 2) * LANES, (h // 2 + 1) * LANES)
        kb = k_ref[0, pl.ds(pl.multiple_of(j * TILE, TILE), TILE), lanes]
        return _dot(kb, qs_ref[i, h])

    def softplus2(z2):
        return jnp.maximum(z2, jnp.log2(1.0 + jnp.exp2(jnp.minimum(z2, EXP2_CLAMP))))

    def suffix(nl):
        return _dot(tri_ref[...], nl.astype(BF16))

    chains = [(i, h, j) for i in range(n_blocks) for h in heads for j in range(i, max(i - 2, -1), -1)]
    n = len(chains)
    zs, nls, lw, sufs, ws, pvs = {}, {}, {}, {}, {}, {}
    for t in range(n + 4):
        if t < n:
            i, h, j = chains[t]
            zs[t] = scores(j, h, i)
        if 0 <= t - 2 < n:
            sufs[t - 2] = suffix(nls.pop(t - 2))
        if 0 <= t - 4 < n:
            c = t - 4
            i, h, j = chains[c]
            pvs[c] = _dot(vt_ref[h, j], ws.pop(c))
        if 0 <= t - 1 < n:
            c = t - 1
            i, h, j = chains[c]
            nl = softplus2(zs[c])
            if j == i:
                nl = jnp.where(past, nl, 0.0)
            lw[c] = zs.pop(c) - nl
            nls[c] = nl
        if 0 <= t - 3 < n:
            c = t - 3
            i, h, j = chains[c]
            if j == i:
                w = jnp.where(past, jnp.exp2(lw.pop(c) + sufs[c][0:TILE]), 0.0)
            else:
                w = jnp.exp2(lw.pop(c) + sufs[c][0:TILE] + sufs[c - 1][TILE:TILE + 1])
            ws[c] = w.astype(BF16)

    by_tile = {}
    for c, (i, h, j) in enumerate(chains):
        by_tile.setdefault((i, h), []).append(c)
    for i in range(n_blocks):
        later = [sum(sufs[c][TILE:TILE + SUBLANES] for c in by_tile[i, h]) for h in heads]
        acc = [sum(pvs[c] for c in by_tile[i, h]) for h in heads]
        _gated_store(o_ref, z_ref, acc, i)
        if i >= 2:
            for h in heads:
                later_ref[i, h] = later[h]
                acc_ref[i, h] = acc[h]
            top_later = functools.reduce(jnp.maximum, later)
            alive_ref[i] = (jnp.max(top_later) > EXP2_UNDERFLOW).astype(jnp.int32)

    def far_tiles(i, _):
        @pl.when(alive_ref[i] > 0)
        def _():
            def cond(carry):
                return (carry[0] >= 0) & (carry[1] > 0)

            def body(carry):
                j, _, later, acc = carry
                new_later, new_acc = [], []
                for h in heads:
                    z = scores(j, h, i)
                    nl = softplus2(z)
                    suf = suffix(nl)
                    w = jnp.exp2((z - nl) + suf[0:TILE] + later[h])
                    new_later.append(later[h] + suf[TILE:TILE + 1])
                    new_acc.append(acc[h] + _dot(vt_ref[h, j], w.astype(BF16)))
                top_later = functools.reduce(jnp.maximum, new_later)
                alive = (jnp.max(top_later) > EXP2_UNDERFLOW).astype(jnp.int32)
                return j - 1, alive, new_later, new_acc

            later = [later_ref[i, h][0:1] for h in heads]
            acc = [acc_ref[i, h] for h in heads]
            _, _, _, acc = lax.while_loop(cond, body, (i - 2, jnp.int32(1), later, acc))
            _gated_store(o_ref, z_ref, acc, i)
        return 0

    lax.fori_loop(2, n_blocks, far_tiles, 0)


def _stick_breaking(proj):
    bsz, seq, _ = proj.shape
    n_blocks = seq // TILE
    tri = jnp.asarray(_suffix_matrix(), BF16)
    return pl.pallas_call(
        _sb_kernel,
        grid=(bsz,),
        in_specs=_branch_specs(seq, 4) + [pl.BlockSpec(tri.shape, lambda b: (0, 0))],
        out_specs=pl.BlockSpec((1, seq, WIDTH), lambda b: (b, 0, 0)),
        out_shape=jax.ShapeDtypeStruct((bsz, seq, WIDTH), BF16),
        scratch_shapes=[
            pltpu.VMEM((N_HEADS, n_blocks, HEAD_DIM, TILE), BF16),
            pltpu.VMEM((n_blocks, N_HEADS, LANES, TILE), BF16),
            pltpu.VMEM((n_blocks, N_HEADS, SUBLANES, TILE), F32),
            pltpu.VMEM((n_blocks, N_HEADS, HEAD_DIM, TILE), F32),
            pltpu.SMEM((n_blocks,), jnp.int32),
        ],
        compiler_params=pltpu.CompilerParams(
            dimension_semantics=("arbitrary",), vmem_limit_bytes=VMEM_LIMIT),
        name="stick_breaking",
    )(proj, proj, proj, proj, tri)


def _merge_kernel(x_ref, ua_ref, ub_ref, nw_ref, gb_ref, wg_ref, wua_ref, wub_ref, wo_ref,
                  o_ref, h_ref, y_ref):
    d = x_ref.shape[1]
    halves = [slice(r, r + PROJ_ROWS // 2) for r in (0, PROJ_ROWS // 2)]
    h_ref[halves[0], :] = _rms_rows(x_ref[halves[0], :], nw_ref[...]).astype(BF16)
    for k, rows in enumerate(halves):
        for c in range(0, d, COL_CHUNK):
            cols = slice(c, c + COL_CHUNK)
            cols_b = slice(d + c, d + c + COL_CHUNK)
            ga = _dot(h_ref[rows, :], wg_ref[:, cols]) + gb_ref[:, cols]
            gb = _dot(h_ref[rows, :], wg_ref[:, cols_b]) + gb_ref[:, cols_b]
            ya = _dot(ua_ref[rows, :], wua_ref[:, cols])
            yb = _dot(ub_ref[rows, :], wub_ref[:, cols])
            y = ya / (1.0 + jnp.exp(-ga)) + yb / (1.0 + jnp.exp(-gb))
            y_ref[rows, cols] = y.astype(BF16)
            if k == 0 and c == 0:
                h_ref[halves[1], :] = _rms_rows(x_ref[halves[1], :], nw_ref[...]).astype(BF16)
    for rows in halves:
        for c in range(0, d, COL_CHUNK):
            cols = slice(c, c + COL_CHUNK)
            o_ref[rows, cols] = x_ref[rows, cols] + _dot(y_ref[rows, :], wo_ref[:, cols])


def _merge(x2d, ua, ub, norm_w, gate_b, w_in, w_up_a, w_up_b, w_out):
    m, d = x2d.shape
    assert ATTN_COLS % (2 * d) == 0
    row_spec = lambda cols: pl.BlockSpec((PROJ_ROWS, cols), lambda i: (i, 0))
    full = lambda a: pl.BlockSpec(a.shape, lambda i: (0, 0))
    gate_cols = pl.BlockSpec((d, 2 * d), lambda i: (0, ATTN_COLS // (2 * d)))
    return pl.pallas_call(
        _merge_kernel,
        grid=(m // PROJ_ROWS,),
        in_specs=[row_spec(d), row_spec(WIDTH), row_spec(WIDTH), full(norm_w), full(gate_b),
                  gate_cols, full(w_up_a), full(w_up_b), full(w_out)],
        out_specs=row_spec(d),
        out_shape=jax.ShapeDtypeStruct((m, d), F32),
        scratch_shapes=[pltpu.VMEM((PROJ_ROWS, d), BF16), pltpu.VMEM((PROJ_ROWS, d), BF16)],
        compiler_params=pltpu.CompilerParams(
            dimension_semantics=("arbitrary",), vmem_limit_bytes=VMEM_LIMIT),
        name="merge",
    )(x2d, ua, ub, norm_w, gate_b, w_in, w_up_a, w_up_b, w_out)


def kernel(x, norm_w, w_in, merge_gate_b, q_norm_w, k_norm_w, rel_bias, w_up_moba, w_up_sb, w_out):
    bsz, seq, d = x.shape
    assert seq % TILE == 0 and (bsz * seq) % PROJ_ROWS == 0 and d % COL_CHUNK == 0
    assert w_in.shape[2] == ATTN_COLS + 2 * d
    rel_bias = rel_bias.astype(F32)
    bias_tiles = _bias_tiles(rel_bias)
    x2d = x.reshape(bsz * seq, d)
    for l in range(norm_w.shape[0]):
        nw = norm_w[l].reshape(1, d).astype(F32)
        w_l = w_in[l].astype(BF16)
        proj = _project(x2d, nw, w_l).reshape(bsz, seq, ATTN_COLS)
        ua = _moba(proj, q_norm_w[l], k_norm_w[l], rel_bias, bias_tiles).reshape(bsz * seq, WIDTH)
        ub = _stick_breaking(proj).reshape(bsz * seq, WIDTH)
        x2d = _merge(x2d, ua, ub, nw, merge_gate_b[l].reshape(1, 2 * d).astype(F32),
                     w_l, w_up_moba[l].astype(BF16), w_up_sb[l].astype(BF16),
                     w_out[l].astype(BF16))
    return x2d.reshape(bsz, seq, d)
```

```python
import functools
import math

import numpy as np
import jax
import jax.numpy as jnp
from jax import lax
from jax.experimental import pallas as pl
from jax.experimental.pallas import tpu as pltpu

F32 = jnp.float32
BF16 = jnp.bfloat16

HEAD_DIM = 64
N_HEADS = 8
WIDTH = N_HEADS * HEAD_DIM
MOBA_BLOCK = 256
MOBA_TOPK = 3
REL_BUCKETS = 32
REL_MAX_DIST = 128
NORM_EPS = 1e-6
NEG_INF = -1e30
SCALE = HEAD_DIM ** -0.5
LOG2_E = math.log2(math.e)
EXP2_UNDERFLOW = -150.0
EXP2_CLAMP = 126.0

LANES = 128
SUBLANES = 8
TILE = MOBA_BLOCK
PAIRS = WIDTH // LANES
ONES_ROWS = 16
ATTN_COLS = 8 * WIDTH
VMEM_LIMIT = 56 * 1024 * 1024
PROJ_ROWS = 1024
COL_CHUNK = 512
SEQ_BLOCKS = 8


def _dot(a, b):
    return jnp.dot(a, b, preferred_element_type=F32)


def _split_bf16(x):
    hi = x.astype(BF16)
    lo = (x - hi.astype(F32)).astype(BF16)
    return hi, lo


def _rms_rows(x, w):
    ms = jnp.mean(x * x, axis=-1, keepdims=True)
    return x * lax.rsqrt(ms + NORM_EPS) * w


def _proj_kernel(x_ref, nw_ref, kw_ref, w_ref, o_ref, kaug_ref, kmean_ref, vta_ref, vtb_ref, h_ref):
    halves = [slice(r, r + PROJ_ROWS // 2) for r in (0, PROJ_ROWS // 2)]
    blocks_per_step = PROJ_ROWS // TILE
    first_blk = (pl.program_id(0) * blocks_per_step) % (SEQ_BLOCKS)
    lane = lax.broadcasted_iota(jnp.int32, (TILE, LANES), 1)
    head0 = lane < HEAD_DIM
    ones = jnp.ones((ONES_ROWS, TILE), BF16)

    def prepare_keys(r):
        rows = slice(r * TILE, (r + 1) * TILE)
        blk = first_blk + r
        hot0 = ((lane == HEAD_DIM + blk) | (lane == HEAD_DIM + SUBLANES + blk)).astype(F32)
        hot1 = ((lane == blk) | (lane == SUBLANES + blk)).astype(F32)
        for p in range(PAIRS):
            kb = o_ref[rows, WIDTH + p * LANES:WIDTH + (p + 1) * LANES].astype(F32)
            sq = kb * kb
            ss_all = jnp.sum(sq, axis=-1, keepdims=True)
            ss0 = jnp.sum(jnp.where(head0, sq, 0.0), axis=-1, keepdims=True)
            r0 = lax.rsqrt(ss0 * (1.0 / HEAD_DIM) + NORM_EPS)
            r1 = lax.rsqrt((ss_all - ss0) * (1.0 / HEAD_DIM) + NORM_EPS)
            khat = kb * jnp.where(head0, r0, r1) * kw_ref[...]
            mean = jnp.mean(khat, axis=0, keepdims=True)
            kmean_ref[r * SUBLANES:(r + 1) * SUBLANES, p * LANES:(p + 1) * LANES] = \
                jnp.broadcast_to(mean, (SUBLANES, LANES))
            kaug_ref[rows, 2 * p * LANES:(2 * p + 1) * LANES] = jnp.where(head0, khat, hot0).astype(BF16)
            kaug_ref[rows, (2 * p + 1) * LANES:(2 * p + 2) * LANES] = jnp.where(head0, hot1, khat).astype(BF16)

    def prepare_values(r, col0, vt_ref, with_ones):
        rows = slice(r * TILE, (r + 1) * TILE)
        for p in range(PAIRS):
            vt = o_ref[rows, col0 + p * LANES:col0 + (p + 1) * LANES].astype(F32).T
            vt_ref[r, 2 * p, 0:HEAD_DIM, :] = vt[0:HEAD_DIM].astype(BF16)
            vt_ref[r, 2 * p + 1, 0:HEAD_DIM, :] = vt[HEAD_DIM:].astype(BF16)
            if with_ones:
                vt_ref[r, 2 * p, HEAD_DIM:, :] = ones
                vt_ref[r, 2 * p + 1, HEAD_DIM:, :] = ones

    after = {1 * COL_CHUNK: lambda r: prepare_keys(r),
             2 * COL_CHUNK: lambda r: prepare_values(r, 2 * WIDTH, vta_ref, True),
             6 * COL_CHUNK: lambda r: prepare_values(r, 6 * WIDTH, vtb_ref, False)}
    h_ref[halves[0], :] = _rms_rows(x_ref[halves[0], :], nw_ref[...]).astype(BF16)
    for k, rows in enumerate(halves):
        pending = []
        for c in range(0, ATTN_COLS, COL_CHUNK):
            o_ref[rows, c:c + COL_CHUNK] = _dot(h_ref[rows, :], w_ref[:, c:c + COL_CHUNK]).astype(BF16)
            if k == 0 and c == 0:
                h_ref[halves[1], :] = _rms_rows(x_ref[halves[1], :], nw_ref[...]).astype(BF16)
            for job in pending:
                job()
            pending = [functools.partial(after[c], 2 * k + s) for s in range(2)] if c in after else []
        for job in pending:
            job()


def _project(x2d, norm_w, k_norm_w, w_in):
    m, d = x2d.shape
    steps = m // PROJ_ROWS
    bps = PROJ_ROWS // TILE
    assert COL_CHUNK == WIDTH and SEQ_BLOCKS % bps == 0 and PROJ_ROWS == 4 * TILE
    kw = jnp.tile(k_norm_w.astype(F32), 2).reshape(1, LANES)
    return pl.pallas_call(
        _proj_kernel,
        grid=(steps,),
        in_specs=[
            pl.BlockSpec((PROJ_ROWS, d), lambda i: (i, 0)),
            pl.BlockSpec((1, d), lambda i: (0, 0)),
            pl.BlockSpec((1, LANES), lambda i: (0, 0)),
            pl.BlockSpec((d, ATTN_COLS), lambda i: (0, 0)),
        ],
        out_specs=[
            pl.BlockSpec((PROJ_ROWS, ATTN_COLS), lambda i: (i, 0)),
            pl.BlockSpec((PROJ_ROWS, N_HEADS * LANES), lambda i: (i, 0)),
            pl.BlockSpec((bps * SUBLANES, WIDTH), lambda i: (i, 0)),
            pl.BlockSpec((bps, N_HEADS, HEAD_DIM + ONES_ROWS, TILE), lambda i: (i, 0, 0, 0)),
            pl.BlockSpec((bps, N_HEADS, HEAD_DIM, TILE), lambda i: (i, 0, 0, 0)),
        ],
        out_shape=[
            jax.ShapeDtypeStruct((m, ATTN_COLS), BF16),
            jax.ShapeDtypeStruct((m, N_HEADS * LANES), BF16),
            jax.ShapeDtypeStruct((steps * bps * SUBLANES, WIDTH), F32),
            jax.ShapeDtypeStruct((steps * bps, N_HEADS, HEAD_DIM + ONES_ROWS, TILE), BF16),
            jax.ShapeDtypeStruct((steps * bps, N_HEADS, HEAD_DIM, TILE), BF16),
        ],
        scratch_shapes=[pltpu.VMEM((PROJ_ROWS, d), BF16)],
        compiler_params=pltpu.CompilerParams(
            dimension_semantics=("arbitrary",), vmem_limit_bytes=VMEM_LIMIT),
        name="proj",
    )(x2d, norm_w, kw, w_in)


def _t5_bucket_np(dist):
    n = np.maximum(dist, 0)
    max_exact = REL_BUCKETS // 2
    nf = np.maximum(n, 1).astype(np.float64)
    large = max_exact + (np.log(nf / max_exact) / math.log(REL_MAX_DIST / max_exact)
                         * (REL_BUCKETS - max_exact)).astype(np.int32)
    large = np.minimum(large, REL_BUCKETS - 1)
    return np.where(n < max_exact, n, large).astype(np.int32)


def _bucket_tiles():
    key = np.arange(TILE)[:, None]
    qry = np.arange(TILE)[None, :]
    own = np.where(qry >= key, _t5_bucket_np(qry - key), REL_BUCKETS)
    prev = _t5_bucket_np(TILE + qry - key)
    assert (_t5_bucket_np(2 * TILE + qry - key) == REL_BUCKETS - 1).all()
    return np.stack([own, prev]).astype(np.int32)


def _bias_kernel(rel_ref, bkt_ref, o_ref):
    h = pl.program_id(0)
    for t in range(2):
        bkt = bkt_ref[t]
        acc = jnp.full((TILE, TILE), NEG_INF, F32)
        for b in range(REL_BUCKETS):
            acc = jnp.where(bkt == b, rel_ref[h, b] * LOG2_E, acc)
        o_ref[0, t] = acc


def _bias_tiles(rel_bias):
    return pl.pallas_call(
        _bias_kernel,
        grid=(N_HEADS,),
        in_specs=[
            pl.BlockSpec(memory_space=pltpu.SMEM),
            pl.BlockSpec((2, TILE, TILE), lambda h: (0, 0, 0)),
        ],
        out_specs=pl.BlockSpec((1, 2, TILE, TILE), lambda h: (h, 0, 0, 0)),
        out_shape=jax.ShapeDtypeStruct((N_HEADS, 2, TILE, TILE), F32),
        compiler_params=pltpu.CompilerParams(dimension_semantics=("arbitrary",)),
        name="bias_tiles",
    )(rel_bias, jnp.asarray(_bucket_tiles()))


def _gated_store(o_ref, z_ref, out_t, tile):
    rows = pl.ds(pl.multiple_of(tile * TILE, TILE), TILE)
    z = z_ref[0, rows, :].astype(F32)
    gate = z / (1.0 + jnp.exp(-z))
    for p in range(PAIRS):
        lanes = slice(p * LANES, (p + 1) * LANES)
        pair_t = jnp.concatenate([out_t[2 * p], out_t[2 * p + 1]], axis=0)
        o_ref[0, rows, lanes] = (pair_t.T * gate[:, lanes]).astype(BF16)


def _interleave(*stage_lists):
    todo = [list(s) for s in stage_lists]
    done = [0] * len(todo)
    while any(done[k] < len(todo[k]) for k in range(len(todo))):
        k = min((k for k in range(len(todo)) if done[k] < len(todo[k])),
                key=lambda k: (done[k] + 0.5) / len(todo[k]))
        todo[k][done[k]]()
        done[k] += 1


def _moba_kernel(rel_ref, q_ref, z_ref, kaug_ref, kmean_ref, vt_ref, qw_ref, bias_ref, o_ref, qaug_ref):
    n_blocks = q_ref.shape[1] // TILE
    pairs = range(PAIRS)

    row = lax.broadcasted_iota(jnp.int32, (LANES, TILE), 0)
    top = row < HEAD_DIM
    lane8 = lax.broadcasted_iota(jnp.int32, (n_blocks, LANES), 1)
    units = [(i, p) for i in range(n_blocks) for p in pairs]
    qts = [q_ref[0, i * TILE:(i + 1) * TILE, p * LANES:(p + 1) * LANES].astype(F32).T
           for i, p in units]
    qhats = []
    for qt in qts:
        sq = qt * qt
        r0 = lax.rsqrt(jnp.sum(sq[0:HEAD_DIM], axis=0, keepdims=True) * (1.0 / HEAD_DIM) + NORM_EPS)
        r1 = lax.rsqrt(jnp.sum(sq[HEAD_DIM:], axis=0, keepdims=True) * (1.0 / HEAD_DIM) + NORM_EPS)
        qhats.append(qt * jnp.where(top, r0, r1) * qw_ref[...])
    gates = []
    for (i, p), qhat in zip(units, qhats):
        km = kmean_ref[0, :, p * LANES:(p + 1) * LANES]
        km2 = jnp.concatenate([jnp.where(lane8 < HEAD_DIM, km, 0.0),
                               jnp.where(lane8 < HEAD_DIM, 0.0, km)], axis=0)
        kh, kl = _split_bf16(km2)
        qh, ql = _split_bf16(qhat)
        gates.append(_dot(kh, qh) + _dot(kh, ql) + _dot(kl, qh))

    def block_terms(g, far_bias, i):
        n = lax.broadcasted_iota(jnp.int32, g.shape, 0)
        cnt = jnp.zeros(g.shape, jnp.int32)
        for m in range(i):
            gm = g[m:m + 1, :]
            beats = (gm > g) | ((gm == g) & (m < n))
            cnt = cnt + jnp.where(beats, 1, 0)
        keep = ((cnt < MOBA_TOPK) & (n < i)) | (n == i)
        hi, lo = _split_bf16(jnp.where(keep, jnp.where(n < i - 1, far_bias, 0.0), NEG_INF))
        return [hi.astype(F32), lo.astype(F32)]

    pad = jnp.zeros((HEAD_DIM - 2 * n_blocks, TILE), F32)
    for (i, p), qhat, gate in zip(units, qhats, gates):
        qs = qhat * (SCALE * LOG2_E)
        far0 = rel_ref[2 * p, REL_BUCKETS - 1] * LOG2_E
        far1 = rel_ref[2 * p + 1, REL_BUCKETS - 1] * LOG2_E
        qaug_ref[i, 2 * p] = jnp.concatenate(
            [qs[0:HEAD_DIM]] + block_terms(gate[0:n_blocks], far0, i) + [pad], axis=0).astype(BF16)
        qaug_ref[i, 2 * p + 1] = jnp.concatenate(
            block_terms(gate[n_blocks:], far1, i) + [pad, qs[HEAD_DIM:]], axis=0).astype(BF16)

    def attend_stages(i):
        s, p, out = {}, {}, {}

        def stage(t):
            if t < N_HEADS:
                qa = qaug_ref[i, t]
                s[t] = []
                for j in range(i + 1):
                    sj = _dot(kaug_ref[0, j * TILE:(j + 1) * TILE, t * LANES:(t + 1) * LANES], qa)
                    if i - j < 2:
                        sj = sj + bias_ref[t, i - j]
                    s[t].append(sj)
            if t >= 2:
                h = t - 2
                acc = None
                for j, pj in enumerate(p.pop(h)):
                    pv = _dot(vt_ref[j, h], pj)
                    acc = pv if acc is None else acc + pv
                out[h] = acc[0:HEAD_DIM] * (1.0 / acc[HEAD_DIM:HEAD_DIM + 1])
            if 1 <= t <= N_HEADS:
                h = t - 1
                m = functools.reduce(jnp.maximum, [jnp.max(sj, axis=0, keepdims=True) for sj in s[h]])
                p[h] = [jnp.exp2(sj - m).astype(BF16) for sj in s.pop(h)]
            if t == N_HEADS + 1:
                _gated_store(o_ref, z_ref, out, i)

        return [functools.partial(stage, t) for t in range(N_HEADS + 2)]

    _interleave(*[attend_stages(i) for i in range(n_blocks)])


def _branch_specs(seq, first):
    return [pl.BlockSpec((1, seq, WIDTH), functools.partial(lambda c, b: (b, 0, c), first + c))
            for c in range(4)]


def _moba(proj, kaug, kmean, vta, q_norm_w, rel_bias, bias_tiles):
    bsz, seq, _ = proj.shape
    n_blocks = seq // TILE
    assert 2 * n_blocks <= HEAD_DIM and n_blocks == SEQ_BLOCKS
    qw = jnp.tile(q_norm_w.astype(F32), 2).reshape(LANES, 1)
    specs = _branch_specs(seq, 0)
    return pl.pallas_call(
        _moba_kernel,
        grid=(bsz,),
        in_specs=[pl.BlockSpec(memory_space=pltpu.SMEM), specs[0], specs[3],
                  pl.BlockSpec((1, seq, N_HEADS * LANES), lambda b: (b, 0, 0)),
                  pl.BlockSpec((1, n_blocks, WIDTH), lambda b: (b, 0, 0)),
                  pl.BlockSpec((n_blocks,) + vta.shape[1:], lambda b: (b, 0, 0, 0)),
                  pl.BlockSpec((LANES, 1), lambda b: (0, 0)),
                  pl.BlockSpec(bias_tiles.shape, lambda b: (0, 0, 0, 0))],
        out_specs=pl.BlockSpec((1, seq, WIDTH), lambda b: (b, 0, 0)),
        out_shape=jax.ShapeDtypeStruct((bsz, seq, WIDTH), BF16),
        scratch_shapes=[pltpu.VMEM((n_blocks, N_HEADS, LANES, TILE), BF16)],
        compiler_params=pltpu.CompilerParams(
            dimension_semantics=("arbitrary",), vmem_limit_bytes=VMEM_LIMIT),
        name="moba",
    )(rel_bias, proj, proj, kaug, kmean, vta, qw, bias_tiles)


def _suffix_matrix():
    s = np.arange(TILE)[:, None]
    j = np.arange(TILE)[None, :]
    return np.concatenate([np.where(j > s, -1.0, 0.0), -np.ones((ONES_ROWS, TILE))]).astype(np.float32)


def _sb_kernel(q_ref, k_ref, z_ref, vt_ref, tri_ref, o_ref,
               qs_ref, later_ref, acc_ref, alive_ref):
    n_blocks = k_ref.shape[1] // TILE
    heads = range(N_HEADS)

    row = lax.broadcasted_iota(jnp.int32, (LANES, TILE), 0)
    for i in range(n_blocks):
        for p in range(PAIRS):
            qt = q_ref[0, i * TILE:(i + 1) * TILE, p * LANES:(p + 1) * LANES].astype(F32).T
            qt = qt * (SCALE * LOG2_E)
            qs_ref[i, 2 * p] = jnp.where(row < HEAD_DIM, qt, 0.0).astype(BF16)
            qs_ref[i, 2 * p + 1] = jnp.where(row < HEAD_DIM, 0.0, qt).astype(BF16)
    key = lax.broadcasted_iota(jnp.int32, (TILE, TILE), 0)
    qry = lax.broadcasted_iota(jnp.int32, (TILE, TILE), 1)
    past = key < qry

    def scores(j, h, i):
        lanes = slice((h // 2) * LANES, (h // 2 + 1) * LANES)
        kb = k_ref[0, pl.ds(pl.multiple_of(j * TILE, TILE), TILE), lanes]
        return _dot(kb, qs_ref[i, h])

    def softplus2(z2):
        return jnp.maximum(z2, jnp.log2(1.0 + jnp.exp2(jnp.minimum(z2, EXP2_CLAMP))))

    def suffix(nl):
        return _dot(tri_ref[...], nl.astype(BF16))

    chains = [(i, h, j) for i in range(n_blocks) for h in heads for j in range(i, max(i - 2, -1), -1)]
    n = len(chains)
    zs, nls, lw, sufs, ws, pvs = {}, {}, {}, {}, {}, {}
    for t in range(n + 4):
        if t < n:
            i, h, j = chains[t]
            zs[t] = scores(j, h, i)
        if 0 <= t - 2 < n:
            sufs[t - 2] = suffix(nls.pop(t - 2))
        if 0 <= t - 4 < n:
            c = t - 4
            i, h, j = chains[c]
            pvs[c] = _dot(vt_ref[j, h], ws.pop(c))
        if 0 <= t - 1 < n:
            c = t - 1
            i, h, j = chains[c]
            nl = softplus2(zs[c])
            if j == i:
                nl = jnp.where(past, nl, 0.0)
            lw[c] = zs.pop(c) - nl
            nls[c] = nl
        if 0 <= t - 3 < n:
            c = t - 3
            i, h, j = chains[c]
            if j == i:
                w = jnp.where(past, jnp.exp2(lw.pop(c) + sufs[c][0:TILE]), 0.0)
            else:
                w = jnp.exp2(lw.pop(c) + sufs[c][0:TILE] + sufs[c - 1][TILE:TILE + 1])
            ws[c] = w.astype(BF16)

    by_tile = {}
    for c, (i, h, j) in enumerate(chains):
        by_tile.setdefault((i, h), []).append(c)
    for i in range(n_blocks):
        later = [sum(sufs[c][TILE:TILE + SUBLANES] for c in by_tile[i, h]) for h in heads]
        acc = [sum(pvs[c] for c in by_tile[i, h]) for h in heads]
        _gated_store(o_ref, z_ref, acc, i)
        if i >= 2:
            for h in heads:
                later_ref[i, h] = later[h]
                acc_ref[i, h] = acc[h]
            top_later = functools.reduce(jnp.maximum, later)
            alive_ref[i] = (jnp.max(top_later) > EXP2_UNDERFLOW).astype(jnp.int32)

    def far_tiles(i, _):
        @pl.when(alive_ref[i] > 0)
        def _():
            def cond(carry):
                return (carry[0] >= 0) & (carry[1] > 0)

            def body(carry):
                j, _, later, acc = carry
                new_later, new_acc = [], []
                for h in heads:
                    z = scores(j, h, i)
                    nl = softplus2(z)
                    suf = suffix(nl)
                    w = jnp.exp2((z - nl) + suf[0:TILE] + later[h])
                    new_later.append(later[h] + suf[TILE:TILE + 1])
                    new_acc.append(acc[h] + _dot(vt_ref[j, h], w.astype(BF16)))
                top_later = functools.reduce(jnp.maximum, new_later)
                alive = (jnp.max(top_later) > EXP2_UNDERFLOW).astype(jnp.int32)
                return j - 1, alive, new_later, new_acc

            later = [later_ref[i, h][0:1] for h in heads]
            acc = [acc_ref[i, h] for h in heads]
            _, _, _, acc = lax.while_loop(cond, body, (i - 2, jnp.int32(1), later, acc))
            _gated_store(o_ref, z_ref, acc, i)
        return 0

    lax.fori_loop(2, n_blocks, far_tiles, 0)


def _stick_breaking(proj, vtb):
    bsz, seq, _ = proj.shape
    n_blocks = seq // TILE
    tri = jnp.asarray(_suffix_matrix(), BF16)
    specs = _branch_specs(seq, 4)
    return pl.pallas_call(
        _sb_kernel,
        grid=(bsz,),
        in_specs=[specs[0], specs[1], specs[3],
                  pl.BlockSpec((n_blocks,) + vtb.shape[1:], lambda b: (b, 0, 0, 0)),
                  pl.BlockSpec(tri.shape, lambda b: (0, 0))],
        out_specs=pl.BlockSpec((1, seq, WIDTH), lambda b: (b, 0, 0)),
        out_shape=jax.ShapeDtypeStruct((bsz, seq, WIDTH), BF16),
        scratch_shapes=[
            pltpu.VMEM((n_blocks, N_HEADS, LANES, TILE), BF16),
            pltpu.VMEM((n_blocks, N_HEADS, SUBLANES, TILE), F32),
            pltpu.VMEM((n_blocks, N_HEADS, HEAD_DIM, TILE), F32),
            pltpu.SMEM((n_blocks,), jnp.int32),
        ],
        compiler_params=pltpu.CompilerParams(
            dimension_semantics=("arbitrary",), vmem_limit_bytes=VMEM_LIMIT),
        name="stick_breaking",
    )(proj, proj, proj, vtb, tri)


def _merge_kernel(x_ref, ua_ref, ub_ref, nw_ref, gb_ref, wg_ref, wua_ref, wub_ref, wo_ref,
                  o_ref, h_ref, y_ref):
    d = x_ref.shape[1]
    halves = [slice(r, r + PROJ_ROWS // 2) for r in (0, PROJ_ROWS // 2)]
    h_ref[halves[0], :] = _rms_rows(x_ref[halves[0], :], nw_ref[...]).astype(BF16)
    for k, rows in enumerate(halves):
        for c in range(0, d, COL_CHUNK):
            cols = slice(c, c + COL_CHUNK)
            cols_b = slice(d + c, d + c + COL_CHUNK)
            ga = _dot(h_ref[rows, :], wg_ref[:, cols]) + gb_ref[:, cols]
            gb = _dot(h_ref[rows, :], wg_ref[:, cols_b]) + gb_ref[:, cols_b]
            ya = _dot(ua_ref[rows, :], wua_ref[:, cols])
            yb = _dot(ub_ref[rows, :], wub_ref[:, cols])
            y = ya / (1.0 + jnp.exp(-ga)) + yb / (1.0 + jnp.exp(-gb))
            y_ref[rows, cols] = y.astype(BF16)
            if k == 0 and c == 0:
                h_ref[halves[1], :] = _rms_rows(x_ref[halves[1], :], nw_ref[...]).astype(BF16)
    for rows in halves:
        for c in range(0, d, COL_CHUNK):
            cols = slice(c, c + COL_CHUNK)
            o_ref[rows, cols] = x_ref[rows, cols] + _dot(y_ref[rows, :], wo_ref[:, cols])


def _merge(x2d, ua, ub, norm_w, gate_b, w_in, w_up_a, w_up_b, w_out):
    m, d = x2d.shape
    assert ATTN_COLS % (2 * d) == 0
    row_spec = lambda cols: pl.BlockSpec((PROJ_ROWS, cols), lambda i: (i, 0))
    full = lambda a: pl.BlockSpec(a.shape, lambda i: (0, 0))
    gate_cols = pl.BlockSpec((d, 2 * d), lambda i: (0, ATTN_COLS // (2 * d)))
    return pl.pallas_call(
        _merge_kernel,
        grid=(m // PROJ_ROWS,),
        in_specs=[row_spec(d), row_spec(WIDTH), row_spec(WIDTH), full(norm_w), full(gate_b),
                  gate_cols, full(w_up_a), full(w_up_b), full(w_out)],
        out_specs=row_spec(d),
        out_shape=jax.ShapeDtypeStruct((m, d), F32),
        scratch_shapes=[pltpu.VMEM((PROJ_ROWS, d), BF16), pltpu.VMEM((PROJ_ROWS, d), BF16)],
        compiler_params=pltpu.CompilerParams(
            dimension_semantics=("arbitrary",), vmem_limit_bytes=VMEM_LIMIT),
        name="merge",
    )(x2d, ua, ub, norm_w, gate_b, w_in, w_up_a, w_up_b, w_out)


def kernel(x, norm_w, w_in, merge_gate_b, q_norm_w, k_norm_w, rel_bias, w_up_moba, w_up_sb, w_out):
    bsz, seq, d = x.shape
    assert seq == SEQ_BLOCKS * TILE and (bsz * seq) % PROJ_ROWS == 0 and d % COL_CHUNK == 0
    assert w_in.shape[2] == ATTN_COLS + 2 * d
    rel_bias = rel_bias.astype(F32)
    bias_tiles = _bias_tiles(rel_bias)
    x2d = x.reshape(bsz * seq, d)
    for l in range(norm_w.shape[0]):
        nw = norm_w[l].reshape(1, d).astype(F32)
        w_l = w_in[l].astype(BF16)
        proj, kaug, kmean, vta, vtb = _project(x2d, nw, k_norm_w[l], w_l)
        proj = proj.reshape(bsz, seq, ATTN_COLS)
        kaug = kaug.reshape(bsz, seq, N_HEADS * LANES)
        kmean = kmean[::SUBLANES].reshape(bsz, seq // TILE, WIDTH)
        ua = _moba(proj, kaug, kmean, vta, q_norm_w[l], rel_bias, bias_tiles).reshape(bsz * seq, WIDTH)
        ub = _stick_breaking(proj, vtb).reshape(bsz * seq, WIDTH)
        x2d = _merge(x2d, ua, ub, nw, merge_gate_b[l].reshape(1, 2 * d).astype(F32),
                     w_l, w_up_moba[l].astype(BF16), w_up_sb[l].astype(BF16),
                     w_out[l].astype(BF16))
    return x2d.reshape(bsz, seq, d)
```

```python
import functools
import math

import numpy as np
import jax
import jax.numpy as jnp
from jax import lax
from jax.experimental import pallas as pl
from jax.experimental.pallas import tpu as pltpu

F32 = jnp.float32
BF16 = jnp.bfloat16

HEAD_DIM = 64
N_HEADS = 8
WIDTH = N_HEADS * HEAD_DIM
MOBA_BLOCK = 256
MOBA_TOPK = 3
REL_BUCKETS = 32
REL_MAX_DIST = 128
NORM_EPS = 1e-6
NEG_INF = -1e30
SCALE = HEAD_DIM ** -0.5
LOG2_E = math.log2(math.e)
EXP2_UNDERFLOW = -150.0
EXP2_CLAMP = 126.0

LANES = 128
SUBLANES = 8
TILE = MOBA_BLOCK
PAIRS = WIDTH // LANES
ONES_ROWS = 16
ATTN_COLS = 8 * WIDTH
VMEM_LIMIT = 56 * 1024 * 1024
PROJ_ROWS = 1024
COL_CHUNK = 512
SEQ_BLOCKS = 8


def _dot(a, b):
    return jnp.dot(a, b, preferred_element_type=F32)


def _split_bf16(x):
    hi = x.astype(BF16)
    lo = (x - hi.astype(F32)).astype(BF16)
    return hi, lo


def _rms_rows(x, w):
    ms = jnp.mean(x * x, axis=-1, keepdims=True)
    return x * lax.rsqrt(ms + NORM_EPS) * w


def _proj_kernel(x_ref, nw_ref, kw_ref, w_ref, o_ref, kaug_ref, kmean_ref, vta_ref, vtb_ref, qsb_ref, h_ref):
    halves = [slice(r, r + PROJ_ROWS // 2) for r in (0, PROJ_ROWS // 2)]
    blocks_per_step = PROJ_ROWS // TILE
    first_blk = (pl.program_id(0) * blocks_per_step) % (SEQ_BLOCKS)
    lane = lax.broadcasted_iota(jnp.int32, (TILE, LANES), 1)
    head0 = lane < HEAD_DIM
    ones = jnp.ones((ONES_ROWS, TILE), BF16)

    def prepare_keys(r):
        rows = slice(r * TILE, (r + 1) * TILE)
        blk = first_blk + r
        hot0 = ((lane == HEAD_DIM + blk) | (lane == HEAD_DIM + SUBLANES + blk)).astype(F32)
        hot1 = ((lane == blk) | (lane == SUBLANES + blk)).astype(F32)
        for p in range(PAIRS):
            kb = o_ref[rows, WIDTH + p * LANES:WIDTH + (p + 1) * LANES].astype(F32)
            sq = kb * kb
            ss_all = jnp.sum(sq, axis=-1, keepdims=True)
            ss0 = jnp.sum(jnp.where(head0, sq, 0.0), axis=-1, keepdims=True)
            r0 = lax.rsqrt(ss0 * (1.0 / HEAD_DIM) + NORM_EPS)
            r1 = lax.rsqrt((ss_all - ss0) * (1.0 / HEAD_DIM) + NORM_EPS)
            khat = kb * jnp.where(head0, r0, r1) * kw_ref[...]
            mean = jnp.mean(khat, axis=0, keepdims=True)
            kmean_ref[r * SUBLANES:(r + 1) * SUBLANES, p * LANES:(p + 1) * LANES] = \
                jnp.broadcast_to(mean, (SUBLANES, LANES))
            kaug_ref[rows, 2 * p * LANES:(2 * p + 1) * LANES] = jnp.where(head0, khat, hot0).astype(BF16)
            kaug_ref[rows, (2 * p + 1) * LANES:(2 * p + 2) * LANES] = jnp.where(head0, hot1, khat).astype(BF16)

    def prepare_values(r, col0, vt_ref, with_ones):
        rows = slice(r * TILE, (r + 1) * TILE)
        for p in range(PAIRS):
            vt = o_ref[rows, col0 + p * LANES:col0 + (p + 1) * LANES].astype(F32).T
            vt_ref[r, 2 * p, 0:HEAD_DIM, :] = vt[0:HEAD_DIM].astype(BF16)
            vt_ref[r, 2 * p + 1, 0:HEAD_DIM, :] = vt[HEAD_DIM:].astype(BF16)
            if with_ones:
                vt_ref[r, 2 * p, HEAD_DIM:, :] = ones
                vt_ref[r, 2 * p + 1, HEAD_DIM:, :] = ones

    def prepare_sb_queries(r):
        rows = slice(r * TILE, (r + 1) * TILE)
        top = lax.broadcasted_iota(jnp.int32, (LANES, TILE), 0) < HEAD_DIM
        for p in range(PAIRS):
            qt = o_ref[rows, 4 * WIDTH + p * LANES:4 * WIDTH + (p + 1) * LANES].astype(F32).T
            qt = qt * (SCALE * LOG2_E)
            qsb_ref[r, 2 * p] = jnp.where(top, qt, 0.0).astype(BF16)
            qsb_ref[r, 2 * p + 1] = jnp.where(top, 0.0, qt).astype(BF16)

    after = {1 * COL_CHUNK: lambda r: prepare_keys(r),
             4 * COL_CHUNK: lambda r: prepare_sb_queries(r),
             2 * COL_CHUNK: lambda r: prepare_values(r, 2 * WIDTH, vta_ref, True),
             6 * COL_CHUNK: lambda r: prepare_values(r, 6 * WIDTH, vtb_ref, False)}
    h_ref[halves[0], :] = _rms_rows(x_ref[halves[0], :], nw_ref[...]).astype(BF16)
    for k, rows in enumerate(halves):
        pending = []
        for c in range(0, ATTN_COLS, COL_CHUNK):
            o_ref[rows, c:c + COL_CHUNK] = _dot(h_ref[rows, :], w_ref[:, c:c + COL_CHUNK]).astype(BF16)
            if k == 0 and c == 0:
                h_ref[halves[1], :] = _rms_rows(x_ref[halves[1], :], nw_ref[...]).astype(BF16)
            for job in pending:
                job()
            pending = [functools.partial(after[c], 2 * k + s) for s in range(2)] if c in after else []
        for job in pending:
            job()


def _project(x2d, norm_w, k_norm_w, w_in):
    m, d = x2d.shape
    steps = m // PROJ_ROWS
    bps = PROJ_ROWS // TILE
    assert COL_CHUNK == WIDTH and SEQ_BLOCKS % bps == 0 and PROJ_ROWS == 4 * TILE
    kw = jnp.tile(k_norm_w.astype(F32), 2).reshape(1, LANES)
    return pl.pallas_call(
        _proj_kernel,
        grid=(steps,),
        in_specs=[
            pl.BlockSpec((PROJ_ROWS, d), lambda i: (i, 0)),
            pl.BlockSpec((1, d), lambda i: (0, 0)),
            pl.BlockSpec((1, LANES), lambda i: (0, 0)),
            pl.BlockSpec((d, ATTN_COLS), lambda i: (0, 0)),
        ],
        out_specs=[
            pl.BlockSpec((PROJ_ROWS, ATTN_COLS), lambda i: (i, 0)),
            pl.BlockSpec((PROJ_ROWS, N_HEADS * LANES), lambda i: (i, 0)),
            pl.BlockSpec((bps * SUBLANES, WIDTH), lambda i: (i, 0)),
            pl.BlockSpec((bps, N_HEADS, HEAD_DIM + ONES_ROWS, TILE), lambda i: (i, 0, 0, 0)),
            pl.BlockSpec((bps, N_HEADS, HEAD_DIM, TILE), lambda i: (i, 0, 0, 0)),
            pl.BlockSpec((bps, N_HEADS, LANES, TILE), lambda i: (i, 0, 0, 0)),
        ],
        out_shape=[
            jax.ShapeDtypeStruct((m, ATTN_COLS), BF16),
            jax.ShapeDtypeStruct((m, N_HEADS * LANES), BF16),
            jax.ShapeDtypeStruct((steps * bps * SUBLANES, WIDTH), F32),
            jax.ShapeDtypeStruct((steps * bps, N_HEADS, HEAD_DIM + ONES_ROWS, TILE), BF16),
            jax.ShapeDtypeStruct((steps * bps, N_HEADS, HEAD_DIM, TILE), BF16),
            jax.ShapeDtypeStruct((steps * bps, N_HEADS, LANES, TILE), BF16),
        ],
        scratch_shapes=[pltpu.VMEM((PROJ_ROWS, d), BF16)],
        compiler_params=pltpu.CompilerParams(
            dimension_semantics=("arbitrary",), vmem_limit_bytes=VMEM_LIMIT),
        name="proj",
    )(x2d, norm_w, kw, w_in)


def _t5_bucket_np(dist):
    n = np.maximum(dist, 0)
    max_exact = REL_BUCKETS // 2
    nf = np.maximum(n, 1).astype(np.float64)
    large = max_exact + (np.log(nf / max_exact) / math.log(REL_MAX_DIST / max_exact)
                         * (REL_BUCKETS - max_exact)).astype(np.int32)
    large = np.minimum(large, REL_BUCKETS - 1)
    return np.where(n < max_exact, n, large).astype(np.int32)


def _bucket_tiles():
    key = np.arange(TILE)[:, None]
    qry = np.arange(TILE)[None, :]
    own = np.where(qry >= key, _t5_bucket_np(qry - key), REL_BUCKETS)
    prev = _t5_bucket_np(TILE + qry - key)
    assert (_t5_bucket_np(2 * TILE + qry - key) == REL_BUCKETS - 1).all()
    return np.stack([own, prev]).astype(np.int32)


def _bias_kernel(rel_ref, bkt_ref, o_ref):
    h = pl.program_id(0)
    for t in range(2):
        bkt = bkt_ref[t]
        acc = jnp.full((TILE, TILE), NEG_INF, F32)
        for b in range(REL_BUCKETS):
            acc = jnp.where(bkt == b, rel_ref[h, b] * LOG2_E, acc)
        o_ref[0, t] = acc


def _bias_tiles(rel_bias):
    return pl.pallas_call(
        _bias_kernel,
        grid=(N_HEADS,),
        in_specs=[
            pl.BlockSpec(memory_space=pltpu.SMEM),
            pl.BlockSpec((2, TILE, TILE), lambda h: (0, 0, 0)),
        ],
        out_specs=pl.BlockSpec((1, 2, TILE, TILE), lambda h: (h, 0, 0, 0)),
        out_shape=jax.ShapeDtypeStruct((N_HEADS, 2, TILE, TILE), F32),
        compiler_params=pltpu.CompilerParams(dimension_semantics=("arbitrary",)),
        name="bias_tiles",
    )(rel_bias, jnp.asarray(_bucket_tiles()))


def _gated_store(o_ref, z_ref, out_t, tile):
    rows = pl.ds(pl.multiple_of(tile * TILE, TILE), TILE)
    z = z_ref[0, rows, :].astype(F32)
    gate = z / (1.0 + jnp.exp(-z))
    for p in range(PAIRS):
        lanes = slice(p * LANES, (p + 1) * LANES)
        pair_t = jnp.concatenate([out_t[2 * p], out_t[2 * p + 1]], axis=0)
        o_ref[0, rows, lanes] = (pair_t.T * gate[:, lanes]).astype(BF16)


def _interleave(*stage_lists):
    todo = [list(s) for s in stage_lists]
    done = [0] * len(todo)
    while any(done[k] < len(todo[k]) for k in range(len(todo))):
        k = min((k for k in range(len(todo)) if done[k] < len(todo[k])),
                key=lambda k: (done[k] + 0.5) / len(todo[k]))
        todo[k][done[k]]()
        done[k] += 1


def _moba_kernel(rel_ref, q_ref, z_ref, kaug_ref, kmean_ref, vt_ref, qw_ref, bias_ref, o_ref, qaug_ref):
    n_blocks = q_ref.shape[1] // TILE
    pairs = range(PAIRS)

    row = lax.broadcasted_iota(jnp.int32, (LANES, TILE), 0)
    top = row < HEAD_DIM
    lane8 = lax.broadcasted_iota(jnp.int32, (n_blocks, LANES), 1)
    units = [(i, p) for i in range(n_blocks) for p in pairs]
    qts = [q_ref[0, i * TILE:(i + 1) * TILE, p * LANES:(p + 1) * LANES].astype(F32).T
           for i, p in units]
    qhats = []
    for qt in qts:
        sq = qt * qt
        r0 = lax.rsqrt(jnp.sum(sq[0:HEAD_DIM], axis=0, keepdims=True) * (1.0 / HEAD_DIM) + NORM_EPS)
        r1 = lax.rsqrt(jnp.sum(sq[HEAD_DIM:], axis=0, keepdims=True) * (1.0 / HEAD_DIM) + NORM_EPS)
        qhats.append(qt * jnp.where(top, r0, r1) * qw_ref[...])
    gates = []
    for (i, p), qhat in zip(units, qhats):
        km = kmean_ref[0, :, p * LANES:(p + 1) * LANES]
        km2 = jnp.concatenate([jnp.where(lane8 < HEAD_DIM, km, 0.0),
                               jnp.where(lane8 < HEAD_DIM, 0.0, km)], axis=0)
        kh, kl = _split_bf16(km2)
        qh, ql = _split_bf16(qhat)
        gates.append(_dot(kh, qh) + _dot(kh, ql) + _dot(kl, qh))

    def block_terms(g, far_bias, i):
        n = lax.broadcasted_iota(jnp.int32, g.shape, 0)
        cnt = jnp.zeros(g.shape, jnp.int32)
        for m in range(i):
            gm = g[m:m + 1, :]
            beats = (gm > g) | ((gm == g) & (m < n))
            cnt = cnt + jnp.where(beats, 1, 0)
        keep = ((cnt < MOBA_TOPK) & (n < i)) | (n == i)
        hi, lo = _split_bf16(jnp.where(keep, jnp.where(n < i - 1, far_bias, 0.0), NEG_INF))
        return [hi.astype(F32), lo.astype(F32)]

    pad = jnp.zeros((HEAD_DIM - 2 * n_blocks, TILE), F32)
    for (i, p), qhat, gate in zip(units, qhats, gates):
        qs = qhat * (SCALE * LOG2_E)
        far0 = rel_ref[2 * p, REL_BUCKETS - 1] * LOG2_E
        far1 = rel_ref[2 * p + 1, REL_BUCKETS - 1] * LOG2_E
        qaug_ref[i, 2 * p] = jnp.concatenate(
            [qs[0:HEAD_DIM]] + block_terms(gate[0:n_blocks], far0, i) + [pad], axis=0).astype(BF16)
        qaug_ref[i, 2 * p + 1] = jnp.concatenate(
            block_terms(gate[n_blocks:], far1, i) + [pad, qs[HEAD_DIM:]], axis=0).astype(BF16)

    def attend_stages(i):
        s, p, out = {}, {}, {}

        def stage(t):
            if t < N_HEADS:
                qa = qaug_ref[i, t]
                s[t] = []
                for j in range(i + 1):
                    sj = _dot(kaug_ref[0, j * TILE:(j + 1) * TILE, t * LANES:(t + 1) * LANES], qa)
                    if i - j < 2:
                        sj = sj + bias_ref[t, i - j]
                    s[t].append(sj)
            if t >= 2:
                h = t - 2
                acc = None
                for j, pj in enumerate(p.pop(h)):
                    pv = _dot(vt_ref[j, h], pj)
                    acc = pv if acc is None else acc + pv
                out[h] = acc[0:HEAD_DIM] * (1.0 / acc[HEAD_DIM:HEAD_DIM + 1])
            if 1 <= t <= N_HEADS:
                h = t - 1
                m = functools.reduce(jnp.maximum, [jnp.max(sj, axis=0, keepdims=True) for sj in s[h]])
                p[h] = [jnp.exp2(sj - m).astype(BF16) for sj in s.pop(h)]
            if t == N_HEADS + 1:
                _gated_store(o_ref, z_ref, out, i)

        return [functools.partial(stage, t) for t in range(N_HEADS + 2)]

    _interleave(*[attend_stages(i) for i in range(n_blocks)])


def _branch_specs(seq, first):
    return [pl.BlockSpec((1, seq, WIDTH), functools.partial(lambda c, b: (b, 0, c), first + c))
            for c in range(4)]


def _moba(proj, kaug, kmean, vta, q_norm_w, rel_bias, bias_tiles):
    bsz, seq, _ = proj.shape
    n_blocks = seq // TILE
    assert 2 * n_blocks <= HEAD_DIM and n_blocks == SEQ_BLOCKS
    qw = jnp.tile(q_norm_w.astype(F32), 2).reshape(LANES, 1)
    specs = _branch_specs(seq, 0)
    return pl.pallas_call(
        _moba_kernel,
        grid=(bsz,),
        in_specs=[pl.BlockSpec(memory_space=pltpu.SMEM), specs[0], specs[3],
                  pl.BlockSpec((1, seq, N_HEADS * LANES), lambda b: (b, 0, 0)),
                  pl.BlockSpec((1, n_blocks, WIDTH), lambda b: (b, 0, 0)),
                  pl.BlockSpec((n_blocks,) + vta.shape[1:], lambda b: (b, 0, 0, 0)),
                  pl.BlockSpec((LANES, 1), lambda b: (0, 0)),
                  pl.BlockSpec(bias_tiles.shape, lambda b: (0, 0, 0, 0))],
        out_specs=pl.BlockSpec((1, seq, WIDTH), lambda b: (b, 0, 0)),
        out_shape=jax.ShapeDtypeStruct((bsz, seq, WIDTH), BF16),
        scratch_shapes=[pltpu.VMEM((n_blocks, N_HEADS, LANES, TILE), BF16)],
        compiler_params=pltpu.CompilerParams(
            dimension_semantics=("arbitrary",), vmem_limit_bytes=VMEM_LIMIT),
        name="moba",
    )(rel_bias, proj, proj, kaug, kmean, vta, qw, bias_tiles)


def _suffix_matrix():
    s = np.arange(TILE)[:, None]
    j = np.arange(TILE)[None, :]
    return np.concatenate([np.where(j > s, -1.0, 0.0), -np.ones((ONES_ROWS, TILE))]).astype(np.float32)


def _sb_kernel(qs_ref, k_ref, z_ref, vt_ref, tri_ref, o_ref,
               later_ref, acc_ref, alive_ref):
    n_blocks = k_ref.shape[1] // TILE
    heads = range(N_HEADS)

    key = lax.broadcasted_iota(jnp.int32, (TILE, TILE), 0)
    qry = lax.broadcasted_iota(jnp.int32, (TILE, TILE), 1)
    past = key < qry

    def scores(j, h, i):
        lanes = slice((h // 2) * LANES, (h // 2 + 1) * LANES)
        kb = k_ref[0, pl.ds(pl.multiple_of(j * TILE, TILE), TILE), lanes]
        return _dot(kb, qs_ref[i, h])

    def softplus2(z2):
        return jnp.maximum(z2, jnp.log2(1.0 + jnp.exp2(jnp.minimum(z2, EXP2_CLAMP))))

    def suffix(nl):
        return _dot(tri_ref[...], nl.astype(BF16))

    chains = [(i, h, j) for i in range(n_blocks) for h in heads for j in range(i, max(i - 2, -1), -1)]
    n = len(chains)
    zs, nls, lw, sufs, ws, pvs = {}, {}, {}, {}, {}, {}
    for t in range(n + 4):
        if t < n:
            i, h, j = chains[t]
            zs[t] = scores(j, h, i)
        if 0 <= t - 2 < n:
            sufs[t - 2] = suffix(nls.pop(t - 2))
        if 0 <= t - 4 < n:
            c = t - 4
            i, h, j = chains[c]
            pvs[c] = _dot(vt_ref[j, h], ws.pop(c))
        if 0 <= t - 1 < n:
            c = t - 1
            i, h, j = chains[c]
            nl = softplus2(zs[c])
            if j == i:
                nl = jnp.where(past, nl, 0.0)
            lw[c] = zs.pop(c) - nl
            nls[c] = nl
        if 0 <= t - 3 < n:
            c = t - 3
            i, h, j = chains[c]
            if j == i:
                w = jnp.where(past, jnp.exp2(lw.pop(c) + sufs[c][0:TILE]), 0.0)
            else:
                w = jnp.exp2(lw.pop(c) + sufs[c][0:TILE] + sufs[c - 1][TILE:TILE + 1])
            ws[c] = w.astype(BF16)

    by_tile = {}
    for c, (i, h, j) in enumerate(chains):
        by_tile.setdefault((i, h), []).append(c)
    for i in range(n_blocks):
        later = [sum(sufs[c][TILE:TILE + SUBLANES] for c in by_tile[i, h]) for h in heads]
        acc = [sum(pvs[c] for c in by_tile[i, h]) for h in heads]
        _gated_store(o_ref, z_ref, acc, i)
        if i >= 2:
            for h in heads:
                later_ref[i, h] = later[h]
                acc_ref[i, h] = acc[h]
            top_later = functools.reduce(jnp.maximum, later)
            alive_ref[i] = (jnp.max(top_later) > EXP2_UNDERFLOW).astype(jnp.int32)

    def far_tiles(i, _):
        @pl.when(alive_ref[i] > 0)
        def _():
            def cond(carry):
                return (carry[0] >= 0) & (carry[1] > 0)

            def body(carry):
                j, _, later, acc = carry
                new_later, new_acc = [], []
                for h in heads:
                    z = scores(j, h, i)
                    nl = softplus2(z)
                    suf = suffix(nl)
                    w = jnp.exp2((z - nl) + suf[0:TILE] + later[h])
                    new_later.append(later[h] + suf[TILE:TILE + 1])
                    new_acc.append(acc[h] + _dot(vt_ref[j, h], w.astype(BF16)))
                top_later = functools.reduce(jnp.maximum, new_later)
                alive = (jnp.max(top_later) > EXP2_UNDERFLOW).astype(jnp.int32)
                return j - 1, alive, new_later, new_acc

            later = [later_ref[i, h][0:1] for h in heads]
            acc = [acc_ref[i, h] for h in heads]
            _, _, _, acc = lax.while_loop(cond, body, (i - 2, jnp.int32(1), later, acc))
            _gated_store(o_ref, z_ref, acc, i)
        return 0

    lax.fori_loop(2, n_blocks, far_tiles, 0)


def _stick_breaking(proj, qsb, vtb):
    bsz, seq, _ = proj.shape
    n_blocks = seq // TILE
    tri = jnp.asarray(_suffix_matrix(), BF16)
    specs = _branch_specs(seq, 4)
    return pl.pallas_call(
        _sb_kernel,
        grid=(bsz,),
        in_specs=[pl.BlockSpec((n_blocks,) + qsb.shape[1:], lambda b: (b, 0, 0, 0)), specs[1], specs[3],
                  pl.BlockSpec((n_blocks,) + vtb.shape[1:], lambda b: (b, 0, 0, 0)),
                  pl.BlockSpec(tri.shape, lambda b: (0, 0))],
        out_specs=pl.BlockSpec((1, seq, WIDTH), lambda b: (b, 0, 0)),
        out_shape=jax.ShapeDtypeStruct((bsz, seq, WIDTH), BF16),
        scratch_shapes=[
            pltpu.VMEM((n_blocks, N_HEADS, SUBLANES, TILE), F32),
            pltpu.VMEM((n_blocks, N_HEADS, HEAD_DIM, TILE), F32),
            pltpu.SMEM((n_blocks,), jnp.int32),
        ],
        compiler_params=pltpu.CompilerParams(
            dimension_semantics=("arbitrary",), vmem_limit_bytes=VMEM_LIMIT),
        name="stick_breaking",
    )(qsb, proj, proj, vtb, tri)


def _merge_kernel(x_ref, ua_ref, ub_ref, nw_ref, gb_ref, wg_ref, wua_ref, wub_ref, wo_ref,
                  o_ref, h_ref, y_ref):
    d = x_ref.shape[1]
    halves = [slice(r, r + PROJ_ROWS // 2) for r in (0, PROJ_ROWS // 2)]
    h_ref[halves[0], :] = _rms_rows(x_ref[halves[0], :], nw_ref[...]).astype(BF16)
    for k, rows in enumerate(halves):
        for c in range(0, d, COL_CHUNK):
            cols = slice(c, c + COL_CHUNK)
            cols_b = slice(d + c, d + c + COL_CHUNK)
            ga = _dot(h_ref[rows, :], wg_ref[:, cols]) + gb_ref[:, cols]
            gb = _dot(h_ref[rows, :], wg_ref[:, cols_b]) + gb_ref[:, cols_b]
            ya = _dot(ua_ref[rows, :], wua_ref[:, cols])
            yb = _dot(ub_ref[rows, :], wub_ref[:, cols])
            y = ya / (1.0 + jnp.exp(-ga)) + yb / (1.0 + jnp.exp(-gb))
            y_ref[rows, cols] = y.astype(BF16)
            if k == 0 and c == 0:
                h_ref[halves[1], :] = _rms_rows(x_ref[halves[1], :], nw_ref[...]).astype(BF16)
    for rows in halves:
        for c in range(0, d, COL_CHUNK):
            cols = slice(c, c + COL_CHUNK)
            o_ref[rows, cols] = x_ref[rows, cols] + _dot(y_ref[rows, :], wo_ref[:, cols])


def _merge(x2d, ua, ub, norm_w, gate_b, w_in, w_up_a, w_up_b, w_out):
    m, d = x2d.shape
    assert ATTN_COLS % (2 * d) == 0
    row_spec = lambda cols: pl.BlockSpec((PROJ_ROWS, cols), lambda i: (i, 0))
    full = lambda a: pl.BlockSpec(a.shape, lambda i: (0, 0))
    gate_cols = pl.BlockSpec((d, 2 * d), lambda i: (0, ATTN_COLS // (2 * d)))
    return pl.pallas_call(
        _merge_kernel,
        grid=(m // PROJ_ROWS,),
        in_specs=[row_spec(d), row_spec(WIDTH), row_spec(WIDTH), full(norm_w), full(gate_b),
                  gate_cols, full(w_up_a), full(w_up_b), full(w_out)],
        out_specs=row_spec(d),
        out_shape=jax.ShapeDtypeStruct((m, d), F32),
        scratch_shapes=[pltpu.VMEM((PROJ_ROWS, d), BF16), pltpu.VMEM((PROJ_ROWS, d), BF16)],
        compiler_params=pltpu.CompilerParams(
            dimension_semantics=("arbitrary",), vmem_limit_bytes=VMEM_LIMIT),
        name="merge",
    )(x2d, ua, ub, norm_w, gate_b, w_in, w_up_a, w_up_b, w_out)


def kernel(x, norm_w, w_in, merge_gate_b, q_norm_w, k_norm_w, rel_bias, w_up_moba, w_up_sb, w_out):
    bsz, seq, d = x.shape
    assert seq == SEQ_BLOCKS * TILE and (bsz * seq) % PROJ_ROWS == 0 and d % COL_CHUNK == 0
    assert w_in.shape[2] == ATTN_COLS + 2 * d
    rel_bias = rel_bias.astype(F32)
    bias_tiles = _bias_tiles(rel_bias)
    x2d = x.reshape(bsz * seq, d)
    for l in range(norm_w.shape[0]):
        nw = norm_w[l].reshape(1, d).astype(F32)
        w_l = w_in[l].astype(BF16)
        proj, kaug, kmean, vta, vtb, qsb = _project(x2d, nw, k_norm_w[l], w_l)
        proj = proj.reshape(bsz, seq, ATTN_COLS)
        kaug = kaug.reshape(bsz, seq, N_HEADS * LANES)
        kmean = kmean[::SUBLANES].reshape(bsz, seq // TILE, WIDTH)
        ua = _moba(proj, kaug, kmean, vta, q_norm_w[l], rel_bias, bias_tiles).reshape(bsz * seq, WIDTH)
        ub = _stick_breaking(proj, qsb, vtb).reshape(bsz * seq, WIDTH)
        x2d = _merge(x2d, ua, ub, nw, merge_gate_b[l].reshape(1, 2 * d).astype(F32),
                     w_l, w_up_moba[l].astype(BF16), w_up_sb[l].astype(BF16),
                     w_out[l].astype(BF16))
    return x2d.reshape(bsz, seq, d)
```
